```python
import math
import jax
import jax.numpy as jnp
from jax import lax
import numpy as np

D_MODEL = 1024
BATCH = 8
SEQ = 4096
DEPTH = 2

CTX_LEN = 256
GRID_W = 64
EPS = 1e-6
ROPE_BASE = 10000.0

DN_HEADS = 6
DN_DK = 64
DN_DV = 64
DN_CONV = 5
DN_CHUNK = 64
DA_HEADS = 4
DA_QK = 32
DA_V = 64
WA_HEADS = 6
WA_KV_HEADS = 2
WA_DIM = 64
WINDOW = 128
BLOCK = 128
BANDK = BLOCK + 2 * WINDOW

A_QK_W = DN_HEADS * DN_DK
A_W = DN_HEADS * DN_DV
B_W = DA_HEADS * DA_V
C_W = WA_HEADS * WA_DIM
MIX_W = A_W + B_W + C_W
A_COLS = 2 * A_QK_W + 2 * A_W + 4 * DN_HEADS
B_COLS = 4 * DA_HEADS * DA_QK + B_W
C_COLS = C_W + 2 * WA_KV_HEADS * WA_DIM
IN_COLS = A_COLS + B_COLS + C_COLS

N_GROUPS = 4
EXP_PER_GROUP = 4
N_EXPERTS = N_GROUPS * EXP_PER_GROUP
TOP_K = 2
D_EXPERT = 512

kernel_name = 'hybrid_deltanet_diffattn_swa_hmoe_dit'


def rms_norm(x, g):
    xf = x.astype(jnp.float32)
    y = xf * lax.rsqrt(jnp.mean(xf * xf, axis=-1, keepdims=True) + EPS)
    return (y * g.astype(jnp.float32)).astype(x.dtype)


def l2_normalize(x):
    return x * lax.rsqrt(jnp.sum(x * x, axis=-1, keepdims=True) + EPS)


def axial_rope_tables(row, col, dim):
    nf = dim // 4
    inv = ROPE_BASE ** (-jnp.arange(nf, dtype=jnp.float32) / nf)
    ang = jnp.stack([row[:, None] * inv, col[:, None] * inv], axis=1)
    return jnp.cos(ang), jnp.sin(ang)


def apply_rope(x, cos, sin):
    b, l, h, dim = x.shape
    xr = x.reshape(b, l, h, 2, 2, dim // 4)
    x1, x2 = xr[..., 0, :], xr[..., 1, :]
    c = cos[None, :, None].astype(x.dtype)
    s = sin[None, :, None].astype(x.dtype)
    return jnp.stack([x1 * c - x2 * s, x2 * c + x1 * s], axis=-2).reshape(b, l, h, dim)


def short_conv(u, w):
    k = w.shape[0]
    return lax.conv_general_dilated(
        u, w[:, None, :].astype(u.dtype), window_strides=(1,), padding=[(k // 2, k // 2)],
        dimension_numbers=('NWC', 'WIO', 'NWC'), feature_group_count=u.shape[-1])


def gated_delta_chunked(q, k, v, g, beta, s0):
    bn, l, h, dk = q.shape
    dv = v.shape[-1]
    C = DN_CHUNK
    n = l // C

    def chunk(t):
        return jnp.moveaxis(t.reshape((bn, n, C) + t.shape[2:]), 3, 1)

    q, k, v, g, beta = chunk(q), chunk(k), chunk(v), chunk(g), chunk(beta)
    gc = jnp.cumsum(g, axis=-1)
    idx = jnp.arange(C)
    tril = idx[:, None] >= idx[None, :]
    strict = idx[:, None] > idx[None, :]
    diff = gc[..., :, None] - gc[..., None, :]
    decay = jnp.where(tril, jnp.exp(jnp.where(tril, diff, 0.0)), 0.0)
    kb = k * beta[..., None]
    a_mat = jnp.where(strict, jnp.einsum('bhncd,bhnsd->bhncs', kb, k) * decay, 0.0)
    eye = jnp.eye(C, dtype=q.dtype)
    rhs = jnp.concatenate([v * beta[..., None], kb * jnp.exp(gc)[..., None]], axis=-1)
    sol = lax.linalg.triangular_solve(eye + a_mat, rhs, left_side=True, lower=True)
    u, w = sol[..., :dv], sol[..., dv:]
    attn = jnp.einsum('bhncd,bhnsd->bhncs', q, k) * decay
    q_dec = q * jnp.exp(gc)[..., None]
    g_last = gc[..., -1]
    k_dec = k * jnp.exp(g_last[..., None] - gc)[..., None]

    def step(S, inp):
        qd, kd, ui, wi, ai, gl = inp
        v_new = ui - jnp.einsum('bhcd,bhde->bhce', wi, S)
        o = jnp.einsum('bhcd,bhde->bhce', qd, S) + jnp.einsum('bhcs,bhse->bhce', ai, v_new)
        S = S * jnp.exp(gl)[..., None, None] + jnp.einsum('bhcd,bhce->bhde', kd, v_new)
        return S, o

    xs = tuple(jnp.moveaxis(t, 2, 0) for t in (q_dec, k_dec, u, w, attn, g_last))
    S, o = lax.scan(step, s0, xs)
    o = jnp.transpose(o, (1, 0, 3, 2, 4)).reshape(bn, l, h, dv)
    return o, S


def dn_prepare(z, conv_w, a_log, dt_bias):
    b, l, _ = z.shape
    h = DN_HEADS
    qkv_w = 2 * A_QK_W + A_W
    qkv = jax.nn.silu(short_conv(z[..., :qkv_w], conv_w)).astype(jnp.float32)
    q = l2_normalize(qkv[..., :A_QK_W].reshape(b, l, h, DN_DK)) * (DN_DK ** -0.5)
    k = l2_normalize(qkv[..., A_QK_W:2 * A_QK_W].reshape(b, l, h, DN_DK))
    v = qkv[..., 2 * A_QK_W:].reshape(b, l, h, DN_DV)
    gate = z[..., qkv_w:qkv_w + A_W]
    ab = z[..., qkv_w + A_W:].astype(jnp.float32).reshape(b, l, 2, 2, h)
    g = -jnp.exp(a_log.astype(jnp.float32)) * jax.nn.softplus(ab[:, :, 0] + dt_bias.astype(jnp.float32))
    beta = jax.nn.sigmoid(ab[:, :, 1])

    def dirs(t_f, t_b):
        return jnp.concatenate([t_f, jnp.flip(t_b, axis=1)], axis=0)

    return (dirs(q, q), dirs(k, k), dirs(v, v), dirs(g[:, :, 0], g[:, :, 1]),
            dirs(beta[:, :, 0], beta[:, :, 1])), gate


def dn_finish(o, gate, norm_g):
    b = o.shape[0] // 2
    l = o.shape[1]
    o = o[:b] + jnp.flip(o[b:], axis=1)
    o = rms_norm(o, norm_g) * jax.nn.silu(gate.astype(jnp.float32).reshape(b, l, DN_HEADS, DN_DV))
    return o.reshape(b, l, A_W).astype(gate.dtype)


def deltanet_mixer(z, zc, conv_w, a_log, dt_bias, norm_g, ctx_out):
    lat, gate = dn_prepare(z, conv_w, a_log, dt_bias)
    cin, gate_c = dn_prepare(zc, conv_w, a_log, dt_bias)
    s0 = jnp.zeros((2 * z.shape[0], DN_HEADS, DN_DK, DN_DV), jnp.float32)
    o_c, s_c = gated_delta_chunked(*cin, s0)
    o, _ = gated_delta_chunked(*lat, s_c)
    y = dn_finish(o, gate, norm_g)
    yc = dn_finish(o_c, gate_c, norm_g) if ctx_out else None
    return y, yc


def diff_core(q1, q2, k1, k2, v, lam):
    scale = DA_QK ** -0.5
    p1 = jax.nn.softmax(jnp.einsum('bhqd,bhkd->bhqk', q1, k1).astype(jnp.float32) * scale, axis=-1)
    p2 = jax.nn.softmax(jnp.einsum('bhqd,bhkd->bhqk', q2, k2).astype(jnp.float32) * scale, axis=-1)
    return jnp.einsum('bhqk,bhkd->bhqd', (p1 - lam * p2).astype(v.dtype), v)


def diff_attention(z, zc, cos, sin, lam_vecs, subln_g, layer_idx, ctx_out):
    b, l, _ = z.shape
    h = DA_HEADS
    qk_w = 2 * h * DA_QK

    def heads(zz, rope):
        bb, n = zz.shape[0], zz.shape[1]
        q = zz[..., :qk_w].reshape(bb, n, 2 * h, DA_QK)
        k = zz[..., qk_w:2 * qk_w].reshape(bb, n, 2 * h, DA_QK)
        v = zz[..., 2 * qk_w:].reshape(bb, n, h, DA_V)
        if rope:
            q = apply_rope(q, cos, sin)
            k = apply_rope(k, cos, sin)

        def halves(t):
            return jnp.transpose(t, (0, 2, 1, 3)).reshape(bb, h, 2, n, DA_QK)

        return halves(q), halves(k), jnp.transpose(v, (0, 2, 1, 3))

    lv = lam_vecs.astype(jnp.float32)
    lam_init = 0.8 - 0.6 * math.exp(-0.3 * layer_idx)
    lam = jnp.exp(jnp.sum(lv[0] * lv[1])) - jnp.exp(jnp.sum(lv[2] * lv[3])) + lam_init
    q, k, v = heads(z, True)
    qc, kc, vc = heads(zc, False)
    k_all = jnp.concatenate([k, kc], axis=3)
    v_all = jnp.concatenate([v, vc], axis=2)
    nb = l // BLOCK
    qb = jnp.moveaxis(q.reshape(b, h, 2, nb, BLOCK, DA_QK), 3, 0)
    o = lax.map(lambda qj: diff_core(qj[:, :, 0], qj[:, :, 1], k_all[:, :, 0], k_all[:, :, 1], v_all, lam), qb)
    o = jnp.moveaxis(o, 0, 2).reshape(b, h, l, DA_V)

    def finish(t):
        t = rms_norm(jnp.transpose(t, (0, 2, 1, 3)), subln_g) * (1.0 - lam_init)
        return t.reshape(t.shape[0], t.shape[1], B_W)

    y = finish(o)
    yc = finish(diff_core(qc[:, :, 0], qc[:, :, 1], kc[:, :, 0], kc[:, :, 1], vc, lam)) if ctx_out else None
    return y, yc


def sink_attend(scores, values, sink):
    s = jnp.concatenate([t.astype(jnp.float32) for t in scores], axis=-1)
    sk = sink[None, :, :, None, None]
    m = jnp.maximum(jnp.max(s, axis=-1, keepdims=True), sk)
    e = jnp.exp(s - m)
    p = e / (jnp.sum(e, axis=-1, keepdims=True) + jnp.exp(sk - m))
    v = jnp.concatenate(values, axis=1)
    return jnp.einsum('bgrqk,bkgd->bqgrd', p.astype(v.dtype), v)


def window_attention(z, zc, cos, sin, sink, ctx_out):
    b, l, _ = z.shape
    g, r = WA_KV_HEADS, WA_HEADS // WA_KV_HEADS
    kv_w = g * WA_DIM
    scale = WA_DIM ** -0.5

    def heads(zz):
        bb, n = zz.shape[0], zz.shape[1]
        q = zz[..., :C_W].reshape(bb, n, WA_HEADS, WA_DIM)
        k = zz[..., C_W:C_W + kv_w].reshape(bb, n, g, WA_DIM)
        v = zz[..., C_W + kv_w:].reshape(bb, n, g, WA_DIM)
        return q, k, v

    q, k, v = heads(z)
    q = apply_rope(q, cos, sin)
    k = apply_rope(k, cos, sin)
    qc, kc, vc = heads(zc)
    sink_f = sink.astype(jnp.float32).reshape(g, r)
    nb = l // BLOCK
    pad = ((0, 0), (WINDOW, WINDOW), (0, 0), (0, 0))
    kp, vp = jnp.pad(k, pad), jnp.pad(v, pad)
    qb = jnp.moveaxis((q * scale).reshape(b, nb, BLOCK, g, r, WA_DIM), 1, 0)
    rel = jnp.arange(BANDK)[None, :] - jnp.arange(BLOCK)[:, None]
    band = (rel >= 0) & (rel <= 2 * WINDOW)

    def block(args):
        qj, j = args
        start = j * BLOCK
        kj = lax.dynamic_slice_in_dim(kp, start, BANDK, axis=1)
        vj = lax.dynamic_slice_in_dim(vp, start, BANDK, axis=1)
        pos = start - WINDOW + jnp.arange(BANDK)
        mask = band & ((pos >= 0) & (pos < l))[None, :]
        s_loc = jnp.where(mask, jnp.einsum('bqgrd,bkgd->bgrqk', qj, kj).astype(jnp.float32), -jnp.inf)
        s_ctx = jnp.einsum('bqgrd,bcgd->bgrqc', qj, kc)
        return sink_attend([s_loc, s_ctx], [vj, vc], sink_f)

    o = lax.map(block, (qb, jnp.arange(nb)))
    y = jnp.moveaxis(o, 0, 1).reshape(b, l, C_W)
    yc = None
    if ctx_out:
        lc = qc.shape[1]
        qcs = (qc * scale).reshape(b, lc, g, r, WA_DIM)
        yc = sink_attend([jnp.einsum('bqgrd,bcgd->bgrqc', qcs, kc)], [vc], sink_f).reshape(b, lc, C_W)
    return y, yc


def hier_moe(h, wg, bg, we, be, w_gate, w_up, w_down):
    lg = (h @ wg).astype(jnp.float32) + bg.astype(jnp.float32)
    pg = jax.nn.softmax(lg, axis=-1)
    oh_g = jax.nn.one_hot(jnp.argmax(lg, axis=-1), N_GROUPS, dtype=jnp.float32)
    p_sel = jnp.sum(pg * oh_g, axis=-1)
    le = ((h @ we).astype(jnp.float32) + be.astype(jnp.float32)).reshape(-1, N_GROUPS, EXP_PER_GROUP)
    le_sel = jnp.einsum('ng,nge->ne', oh_g, le)
    top_v, top_i = lax.top_k(le_sel, TOP_K)
    top_w = jax.nn.softmax(top_v, axis=-1)
    within = jnp.einsum('nk,nke->ne', top_w, jax.nn.one_hot(top_i, EXP_PER_GROUP, dtype=jnp.float32))
    gate = ((oh_g * p_sel[:, None])[:, :, None] * within[:, None, :]).reshape(-1, N_EXPERTS).astype(h.dtype)
    out = jnp.zeros_like(h)
    for e in range(N_EXPERTS):
        hid = jax.nn.silu(h @ w_gate[e]) * (h @ w_up[e])
        out = out + gate[:, e, None] * (hid @ w_down[e])
    return out


def setup_inputs(seed: int = 0) -> dict:
    key = jax.random.key(seed)
    ks = jax.random.split(key, 32)
    f32 = jnp.float32

    def nrm(k, shape, s):
        return jax.random.normal(k, shape, f32) * s

    dt = jnp.exp(jax.random.uniform(ks[10], (DEPTH, 2, DN_HEADS), f32, math.log(1e-3), math.log(1e-1)))
    return {
        'x': nrm(ks[0], (BATCH, SEQ, D_MODEL), 1.0),
        'c': nrm(ks[1], (BATCH, D_MODEL), 1.0),
        'ctx': nrm(ks[2], (BATCH, CTX_LEN, D_MODEL), 1.0),
        'c_ctx': nrm(ks[3], (D_MODEL,), 1.0),
        'ada_w': nrm(ks[4], (DEPTH, D_MODEL, 6 * D_MODEL), 0.5 * D_MODEL ** -0.5),
        'ada_b': nrm(ks[5], (DEPTH, 6 * D_MODEL), 0.02),
        'norm1_g': 1.0 + nrm(ks[6], (DEPTH, D_MODEL), 0.05),
        'norm2_g': 1.0 + nrm(ks[7], (DEPTH, D_MODEL), 0.05),
        'w_in': nrm(ks[8], (DEPTH, D_MODEL, IN_COLS), D_MODEL ** -0.5),
        'dn_conv_w': nrm(ks[9], (DEPTH, DN_CONV, 2 * A_QK_W + A_W), DN_CONV ** -0.5),
        'dn_a_log': jnp.log(jax.random.uniform(ks[11], (DEPTH, 2, DN_HEADS), f32, 1.0, 16.0)),
        'dn_dt_bias': dt + jnp.log(-jnp.expm1(-dt)),
        'dn_norm_g': 1.0 + nrm(ks[12], (DEPTH, DN_DV), 0.05),
        'da_lambda': nrm(ks[13], (DEPTH, 4, DA_QK), 0.1),
        'da_subln_g': 1.0 + nrm(ks[14], (DEPTH, DA_V), 0.05),
        'wa_sink': nrm(ks[15], (DEPTH, WA_HEADS), 1.0),
        'w_out': nrm(ks[16], (DEPTH, MIX_W, D_MODEL), MIX_W ** -0.5),
        'router_group_w': nrm(ks[17], (DEPTH, D_MODEL, N_GROUPS), D_MODEL ** -0.5),
        'router_group_b': nrm(ks[18], (DEPTH, N_GROUPS), 0.01),
        'router_expert_w': nrm(ks[19], (DEPTH, D_MODEL, N_EXPERTS), D_MODEL ** -0.5),
        'router_expert_b': nrm(ks[20], (DEPTH, N_EXPERTS), 0.01),
        'exp_w_gate': nrm(ks[21], (DEPTH, N_EXPERTS, D_MODEL, D_EXPERT), D_MODEL ** -0.5),
        'exp_w_up': nrm(ks[22], (DEPTH, N_EXPERTS, D_MODEL, D_EXPERT), D_MODEL ** -0.5),
        'exp_w_down': nrm(ks[23], (DEPTH, N_EXPERTS, D_EXPERT, D_MODEL), D_EXPERT ** -0.5),
        'final_norm_g': 1.0 + nrm(ks[24], (D_MODEL,), 0.05),
    }


def reference(x, c, ctx, c_ctx, ada_w, ada_b, norm1_g, norm2_g, w_in, dn_conv_w, dn_a_log, dn_dt_bias,
              dn_norm_g, da_lambda, da_subln_g, wa_sink, w_out, router_group_w, router_group_b,
              router_expert_w, router_expert_b, exp_w_gate, exp_w_up, exp_w_down, final_norm_g):
    b, l, d = x.shape
    lc = ctx.shape[1]
    rows = l // GRID_W
    row = jnp.repeat(jnp.arange(rows, dtype=jnp.float32), GRID_W)
    col = jnp.tile(jnp.arange(GRID_W, dtype=jnp.float32), rows)
    cos_b, sin_b = axial_rope_tables(row, col, DA_QK)
    cos_w, sin_w = axial_rope_tables(row, col, WA_DIM)
    s1, s2 = A_COLS, A_COLS + B_COLS
    xc = ctx
    for li in range(DEPTH):
        last = li == DEPTH - 1
        sh1, sc1, g1, sh2, sc2, g2 = jnp.split((jax.nn.silu(c) @ ada_w[li] + ada_b[li])[:, None, :], 6, axis=-1)
        sh1c, sc1c, g1c, sh2c, sc2c, g2c = jnp.split(jax.nn.silu(c_ctx) @ ada_w[li] + ada_b[li], 6, axis=-1)
        h = rms_norm(x, norm1_g[li]) * (1.0 + sc1) + sh1
        hc = rms_norm(xc, norm1_g[li]) * (1.0 + sc1c) + sh1c
        z = h @ w_in[li]
        zc = hc @ w_in[li]
        ya, yac = deltanet_mixer(z[..., :s1], zc[..., :s1], dn_conv_w[li], dn_a_log[li], dn_dt_bias[li],
                                 dn_norm_g[li], not last)
        yb, ybc = diff_attention(z[..., s1:s2], zc[..., s1:s2], cos_b, sin_b, da_lambda[li], da_subln_g[li],
                                 li, not last)
        yw, ywc = window_attention(z[..., s2:], zc[..., s2:], cos_w, sin_w, wa_sink[li], not last)
        x = x + g1 * (jnp.concatenate([ya, yb, yw], axis=-1) @ w_out[li])
        h2 = rms_norm(x, norm2_g[li]) * (1.0 + sc2) + sh2
        moe_w = (router_group_w[li], router_group_b[li], router_expert_w[li], router_expert_b[li],
                 exp_w_gate[li], exp_w_up[li], exp_w_down[li])
        if not last:
            xc = xc + g1c * (jnp.concatenate([yac, ybc, ywc], axis=-1) @ w_out[li])
            h2c = rms_norm(xc, norm2_g[li]) * (1.0 + sc2c) + sh2c
            tokens = jnp.concatenate([h2.reshape(b * l, d), h2c.reshape(b * lc, d)], axis=0)
            ff = hier_moe(tokens, *moe_w)
            x = x + g2 * ff[:b * l].reshape(b, l, d)
            xc = xc + g2c * ff[b * l:].reshape(b, lc, d)
        else:
            x = x + g2 * hier_moe(h2.reshape(b * l, d), *moe_w).reshape(b, l, d)
    return rms_norm(x, final_norm_g)
```

```python
import functools
import math

import jax
import jax.numpy as jnp
from jax import lax
from jax.experimental import pallas as pl
from jax.experimental.pallas import tpu as pltpu

F32 = jnp.float32
BF16 = jnp.bfloat16

GRID_W = 64
EPS = 1e-6
ROPE_BASE = 10000.0
DN_HEADS = 6
DN_DK = 64
DN_DV = 64
DN_CONV = 5
DN_CHUNK = 64
DA_HEADS = 4
DA_QK = 32
DA_V = 64
WA_HEADS = 6
WA_KV_HEADS = 2
WA_DIM = 64
WINDOW = 128
N_GROUPS = 4
EXP_PER_GROUP = 4
N_EXPERTS = 16
D_EXPERT = 512

A_QK_W = DN_HEADS * DN_DK
A_W = DN_HEADS * DN_DV
QKV_W = 2 * A_QK_W + A_W
B_W = DA_HEADS * DA_V
B_QK_W = 2 * DA_HEADS * DA_QK
C_W = WA_HEADS * WA_DIM
C_KV_W = WA_KV_HEADS * WA_DIM
A_COLS = QKV_W + A_W + 4 * DN_HEADS
B_COLS = 2 * B_QK_W + B_W

LANE = 128
TM = 256
MOE_T = 1024
MOE_CH = 128
ROUTE_BLK = 256
VMEM_LIMIT = 56 * 1024 * 1024

_O_QKV = 0
_O_GATE = _O_QKV + QKV_W
_O_AB = _O_GATE + A_W
_O_BQ = _O_AB + LANE
_O_BK = _O_BQ + B_QK_W
_O_BV = _O_BK + B_QK_W
_O_BQS = _O_BV + B_W
_O_BKS = _O_BQS + B_QK_W
_O_CQ = _O_BKS + B_QK_W
_O_CK = _O_CQ + C_W
_O_CV = _O_CK + C_KV_W
_O_CQS = _O_CV + C_KV_W
_O_CKS = _O_CQS + C_W
_O_END = _O_CKS + C_KV_W


def _dot(a, b):
    return jnp.dot(a, b, preferred_element_type=F32)


def _dot_nt(a, b):
    return lax.dot_general(a, b, (((1,), (1,)), ((), ())), preferred_element_type=F32)


def _dot_tn(a, b):
    return lax.dot_general(a, b, (((0,), (0,)), ((), ())), preferred_element_type=F32)


def _split2(a):
    hi = a.astype(BF16)
    lo = (a - hi.astype(F32)).astype(BF16)
    return hi, lo


def _split3(a):
    hi = a.astype(BF16)
    r = a - hi.astype(F32)
    mid = r.astype(BF16)
    lo = (r - mid.astype(F32)).astype(BF16)
    return hi, mid, lo


def _dot3(a, b):
    ah, al = _split2(a)
    bh, bl = _split2(b)
    return _dot(ah, bh) + (_dot(ah, bl) + _dot(al, bh))


def _dot_exact_rhs(a, b_bf16, parts=3):
    sp = _split3(a) if parts == 3 else _split2(a)
    out = _dot(sp[0], b_bf16)
    for p in sp[1:]:
        out = out + _dot(p, b_bf16)
    return out


def _dot_exact_lhs(a_bf16, b, parts=3):
    sp = _split3(b) if parts == 3 else _split2(b)
    out = _dot(a_bf16, sp[0])
    for p in sp[1:]:
        out = out + _dot(a_bf16, p)
    return out


def _silu(x):
    return x * jax.nn.sigmoid(x)


def _softplus(x):
    return jnp.maximum(x, 0.0) + jnp.log1p(jnp.exp(-jnp.abs(x)))


def _params(sem):
    return pltpu.CompilerParams(dimension_semantics=sem, vmem_limit_bytes=VMEM_LIMIT)


def _ada_kernel(c_ref, w_ref, b_ref, o_ref):
    o_ref[...] = _dot3(_silu(c_ref[...]), w_ref[...]) + b_ref[...]


def _ada(cc, w, b):
    rows, d = cc.shape
    n = w.shape[1]
    tn = n // 4
    return pl.pallas_call(
        _ada_kernel,
        grid=(n // tn,),
        in_specs=[pl.BlockSpec((rows, d), lambda j: (0, 0)),
                  pl.BlockSpec((d, tn), lambda j: (0, j)),
                  pl.BlockSpec((1, tn), lambda j: (0, j))],
        out_specs=pl.BlockSpec((rows, tn), lambda j: (0, j)),
        out_shape=jax.ShapeDtypeStruct((rows, n), F32),
        compiler_params=_params(("arbitrary",)),
        name="ada_mod",
    )(cc, w, b.reshape(1, n))


def _mod_spec_d(k, nb, nl, d):
    return pl.BlockSpec((None, None, 1, d), lambda b, i: (jnp.where(i == nl, nb, b), k, 0, 0))


def _rms_mod(x, g, sc, sh):
    y = x * lax.rsqrt(jnp.mean(x * x, axis=-1, keepdims=True) + EPS)
    return (y * g) * (1.0 + sc) + sh


def _in_proj_kernel(fuse_res, *refs):
    if fuse_res:
        xn_ref, ff_ref, g2_ref = refs[:3]
        refs = refs[3:]
    else:
        x_ref = refs[0]
        refs = refs[1:]
    (sc_ref, sh_ref, g_ref, w_ref, cosb_ref, sinb_ref, cosc_ref, sinc_ref) = refs[:8]
    outs = refs[8:]
    if fuse_res:
        x = xn_ref[...] + g2_ref[...] * ff_ref[...]
        outs[0][...] = x
        outs = outs[1:]
    else:
        x = x_ref[...]
    (zqkv_o, gate_o, ab_o, qb_o, kb_o, vb_o, qc_o, kc_o, vc_o) = outs
    hb = _rms_mod(x, g_ref[...], sc_ref[...], sh_ref[...]).astype(BF16)

    def seg(a, b):
        return _dot(hb, w_ref[:, a:b])

    zqkv_o[...] = seg(_O_QKV, _O_GATE)
    gate_o[...] = seg(_O_GATE, _O_AB)
    ab_o[...] = seg(_O_AB, _O_BQ)
    cb = cosb_ref[...]
    sb = sinb_ref[...]
    qb_o[...] = ((seg(_O_BQ, _O_BK) * cb + seg(_O_BQS, _O_BKS) * sb) * (DA_QK ** -0.5)).astype(BF16)
    kb_o[...] = (seg(_O_BK, _O_BV) * cb + seg(_O_BKS, _O_CQ) * sb).astype(BF16)
    vb_o[...] = seg(_O_BV, _O_BQS).astype(BF16)
    cc = cosc_ref[...]
    sc_ = sinc_ref[...]
    cc3 = jnp.concatenate([cc, cc, cc], axis=1)
    sc3 = jnp.concatenate([sc_, sc_, sc_], axis=1)
    qc_o[...] = ((seg(_O_CQ, _O_CK) * cc3 + seg(_O_CQS, _O_CKS) * sc3) * (WA_DIM ** -0.5)).astype(BF16)
    kc_o[...] = (seg(_O_CK, _O_CV) * cc + seg(_O_CKS, _O_END) * sc_).astype(BF16)
    vc_o[...] = seg(_O_CV, _O_CQS).astype(BF16)


def _in_proj(x_parts, mod, norm_g, w_cat, tabs, nb, ntot):
    fuse_res = len(x_parts) == 2
    d = x_parts[0].shape[-1]
    nt = ntot // TM
    nl = nt - 1
    row = lambda w: pl.BlockSpec((None, TM, w), lambda b, i: (b, i, 0))
    tab = lambda w: pl.BlockSpec((TM, w), lambda b, i: (i, 0))
    const = lambda a: pl.BlockSpec(a.shape, lambda b, i: (0,) * a.ndim)
    if fuse_res:
        (xn, ff), prev_mod = x_parts, mod[1]
        ins = [xn, ff, prev_mod]
        in_specs = [row(d), row(d), _mod_spec_d(5, nb, nl, d)]
        cur_mod = mod[0]
    else:
        ins = [x_parts[0]]
        in_specs = [row(d)]
        cur_mod = mod[0]
    ins += [cur_mod, cur_mod, norm_g, w_cat, *tabs]
    in_specs += [_mod_spec_d(1, nb, nl, d), _mod_spec_d(0, nb, nl, d), const(norm_g), const(w_cat),
                 tab(B_QK_W), tab(B_QK_W), tab(C_KV_W), tab(C_KV_W)]
    widths = [(QKV_W, F32), (A_W, F32), (LANE, F32), (B_QK_W, BF16), (B_QK_W, BF16), (B_W, BF16),
              (C_W, BF16), (C_KV_W, BF16), (C_KV_W, BF16)]
    if fuse_res:
        widths = [(d, F32)] + widths
    return pl.pallas_call(
        functools.partial(_in_proj_kernel, fuse_res),
        grid=(nb, nt),
        in_specs=in_specs,
        out_specs=[row(w) for w, _ in widths],
        out_shape=[jax.ShapeDtypeStruct((nb, ntot, w), dt) for w, dt in widths],
        compiler_params=_params(("parallel", "arbitrary")),
        name="in_proj",
    )(*ins)


def _dn_prep_kernel(zc_ref, zp_ref, zn_ref, w_ref, ones_ref, q_o, k_o, v_o, ext_ref):
    i = pl.program_id(1)
    nl = pl.num_programs(1) - 1
    prev_ok = jnp.logical_and(i >= 1, i <= nl - 1)
    next_ok = i <= nl - 2
    ext_ref[0:8, :] = jnp.where(prev_ok, zp_ref[...], 0.0)
    ext_ref[8:8 + TM, :] = zc_ref[...]
    ext_ref[8 + TM:16 + TM, :] = jnp.where(next_ok, zn_ref[...], 0.0)
    half = DN_CONV // 2
    acc = w_ref[0:1, :] * ext_ref[8 - half:8 - half + TM, :]
    for j in range(1, DN_CONV):
        acc = acc + w_ref[j:j + 1, :] * ext_ref[8 - half + j:8 - half + j + TM, :]
    y = _silu(acc)
    ones = ones_ref[...]

    def l2n(t):
        ss = _dot_exact_rhs(t * t, ones, parts=2)
        return t * lax.rsqrt(ss + EPS)

    q_o[...] = l2n(y[:, :A_QK_W]) * (DN_DK ** -0.5)
    k_o[...] = l2n(y[:, A_QK_W:2 * A_QK_W])
    v_o[...] = y[:, 2 * A_QK_W:]


def _dn_prep(zqkv, conv_w8, ones_a):
    nb, ntot, w = zqkv.shape
    nt = ntot // TM
    r8 = TM // 8
    row = lambda ww: pl.BlockSpec((None, TM, ww), lambda b, i: (b, i, 0))
    return pl.pallas_call(
        _dn_prep_kernel,
        grid=(nb, nt),
        in_specs=[row(w),
                  pl.BlockSpec((None, 8, w), lambda b, i: (b, jnp.maximum(i * r8 - 1, 0), 0)),
                  pl.BlockSpec((None, 8, w), lambda b, i: (b, jnp.minimum(i * r8 + r8, ntot // 8 - 1), 0)),
                  pl.BlockSpec(conv_w8.shape, lambda b, i: (0, 0)),
                  pl.BlockSpec(ones_a.shape, lambda b, i: (0, 0))],
        out_specs=[row(A_QK_W), row(A_QK_W), row(A_W)],
        out_shape=[jax.ShapeDtypeStruct((nb, ntot, A_QK_W), F32)] * 3,
        scratch_shapes=[pltpu.VMEM((TM + 16, w), F32)],
        compiler_params=_params(("parallel", "arbitrary")),
        name="dn_prep",
    )(zqkv, zqkv, zqkv, conv_w8, ones_a)


def _dn_chunk(d, rows, q_ref, k_ref, v_ref, ab_ref, o_ref, s_ref, alog, dtb, eg_ref, eb_ref):
    c = DN_CHUNK
    q = q_ref[rows, :]
    k = k_ref[rows, :]
    v = v_ref[rows, :]
    ab = ab_ref[rows, :]
    g = -jnp.exp(alog) * _softplus(ab + dtb)
    beta = jax.nn.sigmoid(ab)
    ri = lax.broadcasted_iota(jnp.int32, (c, c), 0)
    ci = lax.broadcasted_iota(jnp.int32, (c, c), 1)
    if d == 0:
        incl, strict = ri >= ci, ri > ci
    else:
        incl, strict = ri <= ci, ri < ci
    cum = jnp.where(incl, 1.0, 0.0).astype(BF16)
    gc = _dot_exact_lhs(cum, g)
    last = c - 1 if d == 0 else 0
    g_last = gc[last:last + 1, :]
    egc = jnp.exp(gc)
    ekd = jnp.exp(g_last - gc)
    egl = jnp.broadcast_to(jnp.exp(g_last), (8, LANE))
    eg = eg_ref[d]
    eb = eb_ref[d]
    beta_x = _dot_exact_rhs(beta, eb)
    gx = _dot_exact_rhs(jnp.concatenate([egc, ekd, gc, egl], axis=0), eg)
    egc_x, ekd_x, gc_x, egl_x = gx[0:c], gx[c:2 * c], gx[2 * c:3 * c], gx[3 * c:3 * c + 1]
    gc_t = gc.T
    kbeta = k * beta_x
    vbeta = v * beta_x
    wrhs = kbeta * egc_x
    qd = q * egc_x
    kd = k * ekd_x
    lane = lax.broadcasted_iota(jnp.int32, (1, LANE), 1)
    lo = lane < DN_DK
    ri2 = lax.broadcasted_iota(jnp.int32, (LANE, LANE), 0)
    ci2 = lax.broadcasted_iota(jnp.int32, (LANE, LANE), 1)
    bdiag = (ri2 < DN_DK) == (ci2 < DN_DK)
    for p in range(DN_HEADS // 2):
        sl = slice(LANE * p, LANE * p + LANE)
        k_s, q_s = k[:, sl], q[:, sl]
        kb_s, vb_s, wr_s = kbeta[:, sl], vbeta[:, sl], wrhs[:, sl]
        gcx_s = gc_x[:, sl]
        k_sb = k_s.astype(BF16)
        xs, attns = [], []
        for j in range(2):
            h = 2 * p + j
            mine = lo if j == 0 else jnp.logical_not(lo)
            gcol = (gcx_s if j == 0 else pltpu.roll(gcx_s, DN_DK, 1))[:, :c]
            grow = gc_t[DN_HEADS * d + h:DN_HEADS * d + h + 1, :]
            diff = gcol - grow
            dec = jnp.where(incl, jnp.exp(jnp.where(incl, diff, 0.0)), 0.0)
            kk = _dot_nt(jnp.where(mine, kb_s, 0.0).astype(BF16), k_sb)
            a = jnp.where(strict, kk * dec, 0.0)
            qk = _dot_nt(jnp.where(mine, q_s, 0.0).astype(BF16), k_sb)
            attns.append((qk * dec).astype(BF16))
            if j == 0:
                rhs = jnp.where(lo, vb_s, pltpu.roll(wr_s, DN_DK, 1))
            else:
                rhs = jnp.where(lo, pltpu.roll(vb_s, DN_DK, 1), wr_s)
            x = rhs - _dot3(a, rhs)
            pw = a
            for _ in range(5):
                pw = _dot3(pw, pw)
                x = x + _dot3(pw, x)
            xs.append(x)
        u = jnp.where(lo, xs[0], pltpu.roll(xs[1], DN_DK, 1))
        w = jnp.where(lo, pltpu.roll(xs[0], DN_DK, 1), xs[1])
        s = s_ref[d * 3 + p]
        sb = s.astype(BF16)
        v_new = u - _dot(w.astype(BF16), sb)
        vn_b = v_new.astype(BF16)
        o = _dot(qd[:, sl].astype(BF16), sb)
        o = o + _dot(attns[0], jnp.where(lo, vn_b, jnp.zeros_like(vn_b)))
        o = o + _dot(attns[1], jnp.where(lo, jnp.zeros_like(vn_b), vn_b))
        upd = _dot(kd[:, sl].T.astype(BF16), vn_b)
        s_ref[d * 3 + p] = s * egl_x[:, sl] + jnp.where(bdiag, upd, 0.0)
        o_ref[rows, sl] = o


def _dn_scan_kernel(alog_ref, dtb_ref, eg_ref, eb_ref,
                    qf, kf, vf, abf, qb, kb, vb, abb, of_ref, ob_ref, s_ref):
    i = pl.program_id(1)

    @pl.when(i == 0)
    def _():
        s_ref[...] = jnp.zeros_like(s_ref)

    alog = alog_ref[...]
    dtb = dtb_ref[...]
    nch = TM // DN_CHUNK

    def body(cidx, carry):
        rf = pl.ds(pl.multiple_of(cidx * DN_CHUNK, DN_CHUNK), DN_CHUNK)
        rb = pl.ds(pl.multiple_of((nch - 1 - cidx) * DN_CHUNK, DN_CHUNK), DN_CHUNK)
        _dn_chunk(0, rf, qf, kf, vf, abf, of_ref, s_ref, alog, dtb, eg_ref, eb_ref)
        _dn_chunk(1, rb, qb, kb, vb, abb, ob_ref, s_ref, alog, dtb, eg_ref, eb_ref)
        return carry

    lax.fori_loop(0, nch, body, 0)


def _dn_scan(q, k, v, ab, alog, dtb, eg, eb):
    nb, ntot, w = q.shape
    nt = ntot // TM
    nl = nt - 1
    fwd = lambda b, i: (b, jnp.where(i == 0, nl, i - 1), 0)
    bwd = lambda b, i: (b, jnp.where(i == 0, nl, nl - i), 0)
    const = lambda a: pl.BlockSpec(a.shape, lambda b, i: (0,) * a.ndim)
    blk = lambda ww, im: pl.BlockSpec((None, TM, ww), im)
    return pl.pallas_call(
        _dn_scan_kernel,
        grid=(nb, nt),
        in_specs=[const(alog), const(dtb), const(eg), const(eb),
                  blk(w, fwd), blk(w, fwd), blk(w, fwd), blk(LANE, fwd),
                  blk(w, bwd), blk(w, bwd), blk(w, bwd), blk(LANE, bwd)],
        out_specs=[blk(w, fwd), blk(w, bwd)],
        out_shape=[jax.ShapeDtypeStruct((nb, ntot, w), F32)] * 2,
        scratch_shapes=[pltpu.VMEM((2 * (DN_HEADS // 2), LANE, LANE), F32)],
        compiler_params=_params(("parallel", "arbitrary")),
        name="dn_scan",
    )(alog, dtb, eg, eb, q, k, v, ab, q, k, v, ab)


def _diff_attn_kernel(lam_init, n_lat, q_ref, k_ref, v_ref, lam_ref, g_ref, ones_ref, o_ref):
    i = pl.program_id(1)
    nl = pl.num_programs(1) - 1
    lv = lam_ref[...]
    lam = (jnp.exp(jnp.sum(lv[0:1] * lv[1:2], axis=-1, keepdims=True))
           - jnp.exp(jnp.sum(lv[2:3] * lv[3:4], axis=-1, keepdims=True)) + lam_init)
    lane = lax.broadcasted_iota(jnp.int32, (1, B_QK_W), 1)

    def run(k, v):
        q = q_ref[...]
        acc = jnp.zeros((TM, B_W), F32)
        for h in range(DA_HEADS):
            halves = []
            for j in range(2):
                lo = 2 * DA_QK * h + DA_QK * j
                m = jnp.logical_and(lane >= lo, lane < lo + DA_QK)
                s = _dot_nt(jnp.where(m, q, jnp.zeros_like(q)), k)
                e = jnp.exp(s - jnp.max(s, axis=-1, keepdims=True))
                den = jnp.sum(e, axis=-1, keepdims=True)
                halves.append(_dot(e.astype(BF16), v) / den)
            hm = jnp.logical_and(lane >= DA_V * h, lane < DA_V * h + DA_V)
            acc = jnp.where(hm, halves[0] - lam * halves[1], acc)
        ms = _dot_exact_rhs(acc * acc, ones_ref[...], parts=2)
        y = (acc * lax.rsqrt(ms + EPS)) * g_ref[...]
        o_ref[...] = (y * (1.0 - lam_init)).astype(BF16)

    @pl.when(i < nl)
    def _():
        run(k_ref[...], v_ref[...])

    @pl.when(i == nl)
    def _():
        run(k_ref[n_lat:, :], v_ref[n_lat:, :])


def _diff_attn(q, k, v, lam_vecs, subln_g, ones_b, lam_init, n_lat):
    nb, ntot, w = q.shape
    nt = ntot // TM
    row = pl.BlockSpec((None, TM, w), lambda b, i: (b, i, 0))
    full = pl.BlockSpec((None, ntot, w), lambda b, i: (b, 0, 0))
    const = lambda a: pl.BlockSpec(a.shape, lambda b, i: (0,) * a.ndim)
    return pl.pallas_call(
        functools.partial(_diff_attn_kernel, lam_init, n_lat),
        grid=(nb, nt),
        in_specs=[row, full, full, const(lam_vecs), const(subln_g), const(ones_b)],
        out_specs=row,
        out_shape=jax.ShapeDtypeStruct((nb, ntot, w), BF16),
        compiler_params=_params(("parallel", "arbitrary")),
        name="diff_attn",
    )(q, k, v, lam_vecs, subln_g, ones_b)


def _win_attn_kernel(n_lat, sink_ref, q_ref, k_ref, v_ref, o_ref):
    i = pl.program_id(1)
    nl = pl.num_programs(1) - 1
    rep = WA_HEADS // WA_KV_HEADS
    lane = lax.broadcasted_iota(jnp.int32, (1, LANE), 1)
    lo = lane < WA_DIM
    rowg = lax.broadcasted_iota(jnp.int32, (rep * TM, 1), 0) // TM

    def run(k_all, v_all, bias):
        outs = []
        for g in range(WA_KV_HEADS):
            mine = lo if g == 0 else jnp.logical_not(lo)
            q3 = jnp.concatenate(
                [jnp.where(mine, q_ref[:, LANE * s:LANE * s + LANE], jnp.zeros((TM, LANE), BF16))
                 for s in range(rep)], axis=0)
            s = _dot_nt(q3, k_all)
            if bias is not None:
                s = s + bias
            sk = jnp.zeros((rep * TM, 1), F32)
            for r in range(rep):
                sk = jnp.where(rowg == r, sink_ref[rep * g + r], sk)
            m = jnp.maximum(jnp.max(s, axis=-1, keepdims=True), sk)
            e = jnp.exp(s - m)
            den = jnp.sum(e, axis=-1, keepdims=True) + jnp.exp(sk - m)
            outs.append(_dot(e.astype(BF16), v_all) / den)
        for s in range(rep):
            o_ref[:, LANE * s:LANE * s + LANE] = jnp.where(
                lo, outs[0][TM * s:TM * s + TM], outs[1][TM * s:TM * s + TM]).astype(BF16)

    kc = k_ref[n_lat:, :]
    vc = v_ref[n_lat:, :]
    band = 3 * TM

    @pl.when(i < nl)
    def _():
        start = pl.multiple_of(jnp.clip((i - 1) * TM, 0, n_lat - band), TM)
        kb = k_ref[pl.ds(start, band), :]
        vb = v_ref[pl.ds(start, band), :]
        qpos = i * TM + lax.broadcasted_iota(jnp.int32, (TM, 1), 0)
        kpos = start + lax.broadcasted_iota(jnp.int32, (1, band), 1)
        near = jnp.where(jnp.abs(qpos - kpos) <= WINDOW, 0.0, -1e30)
        bias = jnp.concatenate([near, jnp.zeros((TM, kc.shape[0]), F32)], axis=1)
        bias = jnp.concatenate([bias] * rep, axis=0)
        run(jnp.concatenate([kb, kc], axis=0), jnp.concatenate([vb, vc], axis=0), bias)

    @pl.when(i == nl)
    def _():
        run(kc, vc, None)


def _win_attn(q, k, v, sink, n_lat):
    nb, ntot, w = q.shape
    nt = ntot // TM
    kw = k.shape[-1]
    row = pl.BlockSpec((None, TM, w), lambda b, i: (b, i, 0))
    full = pl.BlockSpec((None, ntot, kw), lambda b, i: (b, 0, 0))
    return pl.pallas_call(
        functools.partial(_win_attn_kernel, n_lat),
        grid=(nb, nt),
        in_specs=[pl.BlockSpec(memory_space=pltpu.SMEM), row, full, full],
        out_specs=row,
        out_shape=jax.ShapeDtypeStruct((nb, ntot, w), BF16),
        compiler_params=_params(("parallel", "arbitrary")),
        name="win_attn",
    )(sink, q, k, v)


def _out_proj_kernel(x_ref, of_ref, ob_ref, gate_ref, yb_ref, yw_ref, wa_ref, wb_ref, wc_ref, dng_ref,
                     ones_ref, g1_ref, sc2_ref, sh2_ref, n2_ref, wr_ref, br_ref, xn_o, h2_o, lg_o):
    o = of_ref[...] + ob_ref[...]
    ms = _dot_exact_rhs(o * o, ones_ref[...], parts=2)
    ya = ((o * lax.rsqrt(ms + EPS)) * dng_ref[...]) * _silu(gate_ref[...])
    y = _dot(ya.astype(BF16), wa_ref[...]) + _dot(yb_ref[...], wb_ref[...]) + _dot(yw_ref[...], wc_ref[...])
    xn = x_ref[...] + g1_ref[...] * y
    xn_o[...] = xn
    h2 = _rms_mod(xn, n2_ref[...], sc2_ref[...], sh2_ref[...])
    h2_o[...] = h2.astype(BF16)
    lg_o[...] = _dot3(h2, wr_ref[...]) + br_ref[...]


def _out_proj(x, of, ob, gate, yb, yw, wa, wb, wc, dng, ones_a, mod, n2g, wr, br, nb, ntot):
    d = x.shape[-1]
    nt = ntot // TM
    nl = nt - 1
    row = lambda w: pl.BlockSpec((None, TM, w), lambda b, i: (b, i, 0))
    const = lambda a: pl.BlockSpec(a.shape, lambda b, i: (0,) * a.ndim)
    return pl.pallas_call(
        _out_proj_kernel,
        grid=(nb, nt),
        in_specs=[row(d), row(A_W), row(A_W), row(A_W), row(B_W), row(C_W),
                  const(wa), const(wb), const(wc), const(dng), const(ones_a),
                  _mod_spec_d(2, nb, nl, d), _mod_spec_d(4, nb, nl, d), _mod_spec_d(3, nb, nl, d),
                  const(n2g), const(wr), const(br)],
        out_specs=[row(d), row(d), row(LANE)],
        out_shape=[jax.ShapeDtypeStruct((nb, ntot, d), F32), jax.ShapeDtypeStruct((nb, ntot, d), BF16),
                   jax.ShapeDtypeStruct((nb, ntot, LANE), F32)],
        compiler_params=_params(("parallel", "arbitrary")),
        name="out_proj",
    )(x, of, ob, gate, yb, yw, wa, wb, wc, dng, ones_a, mod, mod, mod, n2g, wr, br)


def _route_kernel(lg_ref, gw_o, pos_o, cnt_o):
    t = lg_ref.shape[0]
    lg = lg_ref[...]
    lane_i = lax.broadcasted_iota(jnp.int32, (1, LANE), 1)
    lane = lane_i.astype(F32)
    big = float(LANE)
    neg = -jnp.inf
    is_g = lane_i < N_GROUPS
    lgm = jnp.where(is_g, lg, neg)
    mg = jnp.max(lgm, axis=-1, keepdims=True)
    p_sel = 1.0 / jnp.sum(jnp.where(is_g, jnp.exp(lgm - mg), 0.0), axis=-1, keepdims=True)
    gidx = jnp.min(jnp.where(jnp.logical_and(is_g, lgm == mg), lane, big), axis=-1, keepdims=True)
    e_lane = lane_i - N_GROUPS
    in_grp = jnp.logical_and(jnp.logical_and(e_lane >= 0, e_lane < N_EXPERTS),
                             jnp.floor((lane - N_GROUPS) * (1.0 / EXP_PER_GROUP)) == gidx)
    le = jnp.where(in_grp, lg, neg)
    v1 = jnp.max(le, axis=-1, keepdims=True)
    i1 = jnp.min(jnp.where(jnp.logical_and(in_grp, le == v1), lane, big), axis=-1, keepdims=True)
    is1 = lane == i1
    le2 = jnp.where(is1, neg, le)
    v2 = jnp.max(le2, axis=-1, keepdims=True)
    rest = jnp.logical_and(in_grp, jnp.logical_not(is1))
    i2 = jnp.min(jnp.where(jnp.logical_and(rest, le2 == v2), lane, big), axis=-1, keepdims=True)
    is2 = lane == i2
    e2 = jnp.exp(v2 - v1)
    w1 = 1.0 / (1.0 + e2)
    w2 = e2 / (1.0 + e2)
    gw = jnp.where(is1, p_sel * w1, jnp.where(is2, p_sel * w2, 0.0))
    sel = jnp.logical_or(is1, is2)
    self_ = jnp.where(sel, 1.0, 0.0)
    ri = lax.broadcasted_iota(jnp.int32, (ROUTE_BLK, ROUTE_BLK), 0)
    ci = lax.broadcasted_iota(jnp.int32, (ROUTE_BLK, ROUTE_BLK), 1)
    tri = jnp.where(ri > ci, 1.0, 0.0).astype(BF16)
    run = jnp.zeros((1, LANE), F32)
    pos_blocks = []
    for b in range(t // ROUTE_BLK):
        blk = self_[b * ROUTE_BLK:(b + 1) * ROUTE_BLK]
        pos_blocks.append(_dot(tri, blk.astype(BF16)) + run)
        run = run + jnp.sum(blk, axis=0, keepdims=True)
    pos = jnp.where(sel, jnp.concatenate(pos_blocks, axis=0), -1.0)
    gw_o[...] = gw.T
    pos_o[...] = pos.T
    cnt_o[...] = jnp.broadcast_to(run, (8, LANE)).astype(jnp.int32)


def _route(logits):
    ntok = logits.shape[0]
    ntile = ntok // MOE_T
    return pl.pallas_call(
        _route_kernel,
        grid=(ntile,),
        in_specs=[pl.BlockSpec((MOE_T, LANE), lambda t: (t, 0))],
        out_specs=[pl.BlockSpec((None, LANE, MOE_T), lambda t: (t, 0, 0)),
                   pl.BlockSpec((None, LANE, MOE_T), lambda t: (t, 0, 0)),
                   pl.BlockSpec((None, 8, LANE), lambda t: (t, 0, 0))],
        out_shape=[jax.ShapeDtypeStruct((ntile, LANE, MOE_T), F32),
                   jax.ShapeDtypeStruct((ntile, LANE, MOE_T), F32),
                   jax.ShapeDtypeStruct((ntile, 8, LANE), jnp.int32)],
        compiler_params=_params(("parallel",)),
        name="moe_route",
    )(logits)


def _moe_kernel(cnt_ref, h_ref, pos_ref, gw_ref, wg_ref, wu_ref, wd_ref, o_ref):
    t = pl.program_id(0)
    e = pl.program_id(1)

    @pl.when(e == 0)
    def _():
        o_ref[...] = jnp.zeros_like(o_ref)

    n = cnt_ref[t * N_EXPERTS + e]
    prow = pos_ref[pl.ds(N_GROUPS + e, 1), :]
    grow = gw_ref[pl.ds(N_GROUPS + e, 1), :]

    def chunk(c, carry):
        slot = (lax.broadcasted_iota(jnp.int32, (MOE_CH, 1), 0) + c * MOE_CH).astype(F32)
        hit = prow == slot
        onehot = jnp.where(hit, 1.0, 0.0).astype(BF16)
        hc = _dot(onehot, h_ref[...]).astype(BF16)
        a = _dot(hc, wg_ref[...])
        b = _dot(hc, wu_ref[...])
        y = _dot((_silu(a) * b).astype(BF16), wd_ref[...])
        gcol = jnp.sum(jnp.where(hit, grow, 0.0), axis=-1, keepdims=True)
        o_ref[...] += _dot_tn(onehot, (y * gcol).astype(BF16))
        return carry

    lax.fori_loop(0, (n + MOE_CH - 1) // MOE_CH, chunk, 0)


def _moe(counts, h2, pos_t, gw_t, wg, wu, wd):
    ntok, d = h2.shape
    ntile = ntok // MOE_T
    gs = pltpu.PrefetchScalarGridSpec(
        num_scalar_prefetch=1,
        grid=(ntile, N_EXPERTS),
        in_specs=[pl.BlockSpec((MOE_T, d), lambda t, e, c: (t, 0)),
                  pl.BlockSpec((None, LANE, MOE_T), lambda t, e, c: (t, 0, 0)),
                  pl.BlockSpec((None, LANE, MOE_T), lambda t, e, c: (t, 0, 0)),
                  pl.BlockSpec((None, d, D_EXPERT), lambda t, e, c: (e, 0, 0)),
                  pl.BlockSpec((None, d, D_EXPERT), lambda t, e, c: (e, 0, 0)),
                  pl.BlockSpec((None, D_EXPERT, d), lambda t, e, c: (e, 0, 0))],
        out_specs=pl.BlockSpec((MOE_T, d), lambda t, e, c: (t, 0)),
    )
    return pl.pallas_call(
        _moe_kernel,
        grid_spec=gs,
        out_shape=jax.ShapeDtypeStruct((ntok, d), F32),
        compiler_params=_params(("parallel", "arbitrary")),
        name="moe_experts",
    )(counts, h2, pos_t, gw_t, wg, wu, wd)


def _final_kernel(xn_ref, ff_ref, g2_ref, g_ref, o_ref):
    x = xn_ref[...] + g2_ref[...] * ff_ref[...]
    y = x * lax.rsqrt(jnp.mean(x * x, axis=-1, keepdims=True) + EPS)
    o_ref[...] = y * g_ref[...]


def _final(xn, ff, mod, g, nb, n_lat, ntot):
    d = xn.shape[-1]
    nl = n_lat // TM
    row = pl.BlockSpec((None, TM, d), lambda b, i: (b, i, 0))
    return pl.pallas_call(
        _final_kernel,
        grid=(nb, nl),
        in_specs=[row, row, _mod_spec_d(5, nb, nl, d), pl.BlockSpec(g.shape, lambda b, i: (0, 0))],
        out_specs=row,
        out_shape=jax.ShapeDtypeStruct((nb, n_lat, d), F32),
        compiler_params=_params(("parallel", "arbitrary")),
        name="final_norm",
    )(xn, ff, mod, g)


def _rope_swap_perm(width, dim):
    nf = dim // 4
    j = jnp.arange(width)
    base = (j // (2 * nf)) * (2 * nf)
    return base + (j % (2 * nf) + nf) % (2 * nf)


def _rope_tables(n_lat, n_ctx, dim, width):
    nf = dim // 4
    t = jnp.arange(n_lat)
    row = (t // GRID_W).astype(F32)
    col = (t % GRID_W).astype(F32)
    inv = ROPE_BASE ** (-jnp.arange(nf, dtype=F32) / nf)
    j = jnp.arange(width) % dim
    axis = j // (2 * nf)
    pos = jnp.where(axis[None, :] == 0, row[:, None], col[:, None])
    ang = pos * inv[j % nf][None, :]
    sign = jnp.where(j % (2 * nf) < nf, -1.0, 1.0).astype(F32)
    cos = jnp.concatenate([jnp.cos(ang), jnp.ones((n_ctx, width), F32)], axis=0)
    sin = jnp.concatenate([jnp.sin(ang) * sign[None, :], jnp.zeros((n_ctx, width), F32)], axis=0)
    return cos, sin


def _block_ones(width, group, value):
    j = jnp.arange(width)
    return jnp.where((j[:, None] // group) == (j[None, :] // group), value, 0.0).astype(BF16)


def _slot_cols():
    rep = WA_HEADS // WA_KV_HEADS
    cols = []
    for s in range(rep):
        cols.append(jnp.arange(WA_DIM) + WA_DIM * s)
        cols.append(jnp.arange(WA_DIM) + WA_DIM * (rep + s))
    return jnp.concatenate(cols)


def _build_w_in(w):
    d = w.shape[0]
    s1, s2 = A_COLS, A_COLS + B_COLS
    wa, wb, wc = w[:, :s1], w[:, s1:s2], w[:, s2:]
    ab = jnp.pad(wa[:, QKV_W + A_W:], ((0, 0), (0, LANE - 4 * DN_HEADS)))
    bq, bk, bv = wb[:, :B_QK_W], wb[:, B_QK_W:2 * B_QK_W], wb[:, 2 * B_QK_W:]
    pb = _rope_swap_perm(B_QK_W, DA_QK)
    cq = wc[:, :C_W][:, _slot_cols()]
    ck, cv = wc[:, C_W:C_W + C_KV_W], wc[:, C_W + C_KV_W:]
    pcq = _rope_swap_perm(C_W, WA_DIM)
    pck = _rope_swap_perm(C_KV_W, WA_DIM)
    cat = jnp.concatenate([wa[:, :QKV_W], wa[:, QKV_W:QKV_W + A_W], ab,
                           bq, bk, bv, bq[:, pb], bk[:, pb],
                           cq, ck, cv, cq[:, pcq], ck[:, pck]], axis=1)
    assert cat.shape == (d, _O_END)
    return cat.astype(BF16)


def _expand_mats():
    r = jnp.arange(LANE)[:, None]
    h = (jnp.arange(A_QK_W) // DN_DK)[None, :]
    eg = jnp.stack([(r == DN_HEADS * d + h) for d in range(2)]).astype(BF16)
    eb = jnp.stack([(r == 2 * DN_HEADS + DN_HEADS * d + h) for d in range(2)]).astype(BF16)
    return eg, eb


def kernel(x, c, ctx, c_ctx, ada_w, ada_b, norm1_g, norm2_g, w_in, dn_conv_w, dn_a_log, dn_dt_bias, dn_norm_g, da_lambda, da_subln_g, wa_sink, w_out, router_group_w, router_group_b, router_expert_w, router_expert_b, exp_w_gate, exp_w_up, exp_w_down, final_norm_g):
    nb, n_lat, d = x.shape
    n_ctx = ctx.shape[1]
    depth = ada_w.shape[0]
    assert n_ctx == TM and n_lat % TM == 0 and n_lat >= 3 * TM
    ntot = n_lat + n_ctx
    ntok = nb * ntot
    assert ntok % MOE_T == 0

    xa = jnp.concatenate([x, ctx], axis=1)
    cc = jnp.zeros((16, d), F32).at[:nb].set(c).at[nb].set(c_ctx)
    cosb, sinb = _rope_tables(n_lat, n_ctx, DA_QK, B_QK_W)
    cosc, sinc = _rope_tables(n_lat, n_ctx, WA_DIM, C_KV_W)
    tabs = (cosb, sinb, cosc, sinc)
    ones_a = _block_ones(A_W, DN_DV, 1.0)
    mean_a = _block_ones(A_W, DN_DV, 1.0 / DN_DV)
    mean_b = _block_ones(B_W, DA_V, 1.0 / DA_V)
    eg, eb = _expand_mats()
    slot_rows = _slot_cols()
    pad_lane = lambda v: jnp.pad(v.reshape(1, -1), ((0, 0), (0, LANE - v.size)))

    mods = [_ada(cc, ada_w[li], ada_b[li]).reshape(16, 6, 1, d) for li in range(depth)]
    xn = ff = None
    for li in range(depth):
        mod = mods[li]
        w_cat = _build_w_in(w_in[li])
        n1g = norm1_g[li].reshape(1, d)
        if li == 0:
            outs = _in_proj((xa,), (mod,), n1g, w_cat, tabs, nb, ntot)
            xcur = xa
        else:
            outs = _in_proj((xn, ff.reshape(nb, ntot, d)), (mod, mods[li - 1]), n1g, w_cat, tabs, nb, ntot)
            xcur, outs = outs[0], outs[1:]
        zqkv, gate, ab, qb, kb, vb, qc, kc, vc = outs

        conv_w8 = jnp.pad(dn_conv_w[li], ((0, 8 - DN_CONV), (0, 0)))
        q, k, v = _dn_prep(zqkv, conv_w8, ones_a)
        of, ob = _dn_scan(q, k, v, ab, pad_lane(dn_a_log[li]), pad_lane(dn_dt_bias[li]), eg, eb)

        lam_init = 0.8 - 0.6 * math.exp(-0.3 * li)
        yb = _diff_attn(qb, kb, vb, da_lambda[li], jnp.tile(da_subln_g[li], DA_HEADS).reshape(1, B_W),
                        mean_b, lam_init, n_lat)
        yw = _win_attn(qc, kc, vc, jnp.pad(wa_sink[li], (0, 8 - WA_HEADS)), n_lat)

        wo = w_out[li]
        wa_o = wo[:A_W].astype(BF16)
        wb_o = wo[A_W:A_W + B_W].astype(BF16)
        wc_o = wo[A_W + B_W:][slot_rows].astype(BF16)
        wr = jnp.pad(jnp.concatenate([router_group_w[li], router_expert_w[li]], axis=1),
                     ((0, 0), (0, LANE - N_GROUPS - N_EXPERTS)))
        br = pad_lane(jnp.concatenate([router_group_b[li], router_expert_b[li]]))
        xn, h2, logits = _out_proj(xcur, of, ob, gate, yb, yw, wa_o, wb_o, wc_o,
                                   jnp.tile(dn_norm_g[li], DN_HEADS).reshape(1, A_W), mean_a, mod,
                                   norm2_g[li].reshape(1, d), wr, br, nb, ntot)

        gw_t, pos_t, cnt = _route(logits.reshape(ntok, LANE))
        counts = cnt[:, 0, N_GROUPS:N_GROUPS + N_EXPERTS].reshape(-1)
        ff = _moe(counts, h2.reshape(ntok, d), pos_t, gw_t, exp_w_gate[li].astype(BF16),
                  exp_w_up[li].astype(BF16), exp_w_down[li].astype(BF16))

    return _final(xn, ff.reshape(nb, ntot, d), mods[depth - 1], final_norm_g.reshape(1, d), nb, n_lat, ntot)
```

```python
import functools
import math

import jax
import jax.numpy as jnp
from jax import lax
from jax.experimental import pallas as pl
from jax.experimental.pallas import tpu as pltpu

F32 = jnp.float32
BF16 = jnp.bfloat16

GRID_W = 64
EPS = 1e-6
LOG2E = math.log2(math.e)
ROPE_BASE = 10000.0
DN_HEADS = 6
DN_DK = 64
DN_DV = 64
DN_CONV = 5
DN_CHUNK = 64
DA_HEADS = 4
DA_QK = 32
DA_V = 64
WA_HEADS = 6
WA_KV_HEADS = 2
WA_DIM = 64
WINDOW = 128
N_GROUPS = 4
EXP_PER_GROUP = 4
N_EXPERTS = 16
D_EXPERT = 512

A_QK_W = DN_HEADS * DN_DK
A_W = DN_HEADS * DN_DV
QKV_W = 2 * A_QK_W + A_W
B_W = DA_HEADS * DA_V
B_QK_W = 2 * DA_HEADS * DA_QK
C_W = WA_HEADS * WA_DIM
C_KV_W = WA_KV_HEADS * WA_DIM
A_COLS = QKV_W + A_W + 4 * DN_HEADS
B_COLS = 2 * B_QK_W + B_W

LANE = 128
TM = 256
MOE_T = 1024
MOE_CH = 128
ROUTE_BLK = 256
VMEM_LIMIT = 56 * 1024 * 1024

_O_QKV = 0
_O_GATE = _O_QKV + QKV_W
_O_AB = _O_GATE + A_W
_O_BQ = _O_AB + LANE
_O_BK = _O_BQ + B_QK_W
_O_BV = _O_BK + B_QK_W
_O_BQS = _O_BV + B_W
_O_BKS = _O_BQS + B_QK_W
_O_CQ = _O_BKS + B_QK_W
_O_CK = _O_CQ + C_W
_O_CV = _O_CK + C_KV_W
_O_CQS = _O_CV + C_KV_W
_O_CKS = _O_CQS + C_W
_O_END = _O_CKS + C_KV_W


def _dot(a, b):
    return jnp.dot(a, b, preferred_element_type=F32)


def _dot_nt(a, b):
    return lax.dot_general(a, b, (((1,), (1,)), ((), ())), preferred_element_type=F32)


def _dot_tn(a, b):
    return lax.dot_general(a, b, (((0,), (0,)), ((), ())), preferred_element_type=F32)


def _split2(a):
    hi = a.astype(BF16)
    lo = (a - hi.astype(F32)).astype(BF16)
    return hi, lo


def _split3(a):
    hi = a.astype(BF16)
    r = a - hi.astype(F32)
    mid = r.astype(BF16)
    lo = (r - mid.astype(F32)).astype(BF16)
    return hi, mid, lo


def _dot3(a, b):
    ah, al = _split2(a)
    bh, bl = _split2(b)
    return _dot(ah, bh) + (_dot(ah, bl) + _dot(al, bh))


def _dot_exact_rhs(a, b_bf16, parts=3):
    sp = _split3(a) if parts == 3 else _split2(a)
    out = _dot(sp[0], b_bf16)
    for p in sp[1:]:
        out = out + _dot(p, b_bf16)
    return out


def _dot_exact_lhs(a_bf16, b, parts=3):
    sp = _split3(b) if parts == 3 else _split2(b)
    out = _dot(a_bf16, sp[0])
    for p in sp[1:]:
        out = out + _dot(a_bf16, p)
    return out


def _col_reduce(x, op, slab=64):
    n, w = x.shape
    if n > slab and n % slab == 0:
        x = op(x.reshape(n // slab, slab, w), axis=0)
    return op(x, axis=0, keepdims=True)


def _silu(x):
    return x * jax.nn.sigmoid(x)


def _softplus(x):
    return jnp.maximum(x, 0.0) + jnp.log1p(jnp.exp(-jnp.abs(x)))


def _params(sem):
    return pltpu.CompilerParams(dimension_semantics=sem, vmem_limit_bytes=VMEM_LIMIT)


def _ada_kernel(c_ref, w_ref, b_ref, o_ref):
    o_ref[...] = _dot3(_silu(c_ref[...]), w_ref[...]) + b_ref[...]


def _ada(cc, w, b):
    rows, d = cc.shape
    n = w.shape[1]
    tn = n // 4
    return pl.pallas_call(
        _ada_kernel,
        grid=(n // tn,),
        in_specs=[pl.BlockSpec((rows, d), lambda j: (0, 0)),
                  pl.BlockSpec((d, tn), lambda j: (0, j)),
                  pl.BlockSpec((1, tn), lambda j: (0, j))],
        out_specs=pl.BlockSpec((rows, tn), lambda j: (0, j)),
        out_shape=jax.ShapeDtypeStruct((rows, n), F32),
        compiler_params=_params(("arbitrary",)),
        name="ada_mod",
    )(cc, w, b.reshape(1, n))


def _mod_spec_d(k, nb, nl, d):
    return pl.BlockSpec((None, None, 1, d), lambda b, i: (jnp.where(i == nl, nb, b), k, 0, 0))


def _rms_mod(x, g, sc, sh):
    y = x * lax.rsqrt(jnp.mean(x * x, axis=-1, keepdims=True) + EPS)
    return (y * g) * (1.0 + sc) + sh


def _in_proj_kernel(fuse_res, *refs):
    if fuse_res:
        xn_ref, ff_ref, g2_ref = refs[:3]
        refs = refs[3:]
    else:
        x_ref = refs[0]
        refs = refs[1:]
    (sc_ref, sh_ref, g_ref, w_ref, cosb_ref, sinb_ref, cosc_ref, sinc_ref) = refs[:8]
    outs = refs[8:]
    if fuse_res:
        x = xn_ref[...] + g2_ref[...] * ff_ref[...]
        outs[0][...] = x
        outs = outs[1:]
    else:
        x = x_ref[...]
    (zqkv_o, gate_o, ab_o, qb_o, kb_o, vb_o, qc_o, kc_o, vc_o) = outs
    hb = _rms_mod(x, g_ref[...], sc_ref[...], sh_ref[...]).astype(BF16)

    def seg(a, b):
        return _dot(hb, w_ref[:, a:b])

    zqkv_o[...] = seg(_O_QKV, _O_GATE)
    gate_o[...] = seg(_O_GATE, _O_AB)
    ab_o[...] = seg(_O_AB, _O_BQ)
    cb = cosb_ref[...]
    sb = sinb_ref[...]
    qb_o[...] = ((seg(_O_BQ, _O_BK) * cb + seg(_O_BQS, _O_BKS) * sb) * (DA_QK ** -0.5 * LOG2E)).astype(BF16)
    kb_o[...] = (seg(_O_BK, _O_BV) * cb + seg(_O_BKS, _O_CQ) * sb).astype(BF16)
    vb_o[...] = seg(_O_BV, _O_BQS).T.astype(BF16)
    cc = cosc_ref[...]
    sc_ = sinc_ref[...]
    cc3 = jnp.concatenate([cc, cc, cc], axis=1)
    sc3 = jnp.concatenate([sc_, sc_, sc_], axis=1)
    qc_o[...] = ((seg(_O_CQ, _O_CK) * cc3 + seg(_O_CQS, _O_CKS) * sc3) * (WA_DIM ** -0.5)).astype(BF16)
    kc_o[...] = (seg(_O_CK, _O_CV) * cc + seg(_O_CKS, _O_END) * sc_).astype(BF16)
    vc_o[...] = seg(_O_CV, _O_CQS).astype(BF16)


def _in_proj(x_parts, mod, norm_g, w_cat, tabs, nb, ntot):
    fuse_res = len(x_parts) == 2
    d = x_parts[0].shape[-1]
    nt = ntot // TM
    nl = nt - 1
    row = lambda w: pl.BlockSpec((None, TM, w), lambda b, i: (b, i, 0))
    tab = lambda w: pl.BlockSpec((TM, w), lambda b, i: (i, 0))
    const = lambda a: pl.BlockSpec(a.shape, lambda b, i: (0,) * a.ndim)
    if fuse_res:
        (xn, ff), prev_mod = x_parts, mod[1]
        ins = [xn, ff, prev_mod]
        in_specs = [row(d), row(d), _mod_spec_d(5, nb, nl, d)]
        cur_mod = mod[0]
    else:
        ins = [x_parts[0]]
        in_specs = [row(d)]
        cur_mod = mod[0]
    ins += [cur_mod, cur_mod, norm_g, w_cat, *tabs]
    in_specs += [_mod_spec_d(1, nb, nl, d), _mod_spec_d(0, nb, nl, d), const(norm_g), const(w_cat),
                 tab(B_QK_W), tab(B_QK_W), tab(C_KV_W), tab(C_KV_W)]
    widths = [(QKV_W, F32), (A_W, F32), (LANE, F32), (B_QK_W, BF16), (B_QK_W, BF16), (None, BF16),
              (C_W, BF16), (C_KV_W, BF16), (C_KV_W, BF16)]
    if fuse_res:
        widths = [(d, F32)] + widths
    return pl.pallas_call(
        functools.partial(_in_proj_kernel, fuse_res),
        grid=(nb, nt),
        in_specs=in_specs,
        out_specs=[pl.BlockSpec((None, None, B_W, TM), lambda b, i: (b, i, 0, 0)) if w is None else row(w)
                   for w, _ in widths],
        out_shape=[jax.ShapeDtypeStruct((nb, nt, B_W, TM) if w is None else (nb, ntot, w), dt)
                   for w, dt in widths],
        compiler_params=_params(("parallel", "arbitrary")),
        name="in_proj",
    )(*ins)


def _dn_prep_kernel(zc_ref, zp_ref, zn_ref, w_ref, ones_ref, q_o, k_o, v_o, ext_ref):
    i = pl.program_id(1)
    nl = pl.num_programs(1) - 1
    prev_ok = jnp.logical_and(i >= 1, i <= nl - 1)
    next_ok = i <= nl - 2
    ext_ref[0:8, :] = jnp.where(prev_ok, zp_ref[...], 0.0)
    ext_ref[8:8 + TM, :] = zc_ref[...]
    ext_ref[8 + TM:16 + TM, :] = jnp.where(next_ok, zn_ref[...], 0.0)
    half = DN_CONV // 2
    acc = w_ref[0:1, :] * ext_ref[8 - half:8 - half + TM, :]
    for j in range(1, DN_CONV):
        acc = acc + w_ref[j:j + 1, :] * ext_ref[8 - half + j:8 - half + j + TM, :]
    y = _silu(acc)
    ones = ones_ref[...]

    def l2n(t):
        ss = _dot_exact_rhs(t * t, ones, parts=2)
        return t * lax.rsqrt(ss + EPS)

    q_o[...] = l2n(y[:, :A_QK_W]) * (DN_DK ** -0.5)
    k_o[...] = l2n(y[:, A_QK_W:2 * A_QK_W])
    v_o[...] = y[:, 2 * A_QK_W:]


def _dn_prep(zqkv, conv_w8, ones_a):
    nb, ntot, w = zqkv.shape
    nt = ntot // TM
    r8 = TM // 8
    row = lambda ww: pl.BlockSpec((None, TM, ww), lambda b, i: (b, i, 0))
    return pl.pallas_call(
        _dn_prep_kernel,
        grid=(nb, nt),
        in_specs=[row(w),
                  pl.BlockSpec((None, 8, w), lambda b, i: (b, jnp.maximum(i * r8 - 1, 0), 0)),
                  pl.BlockSpec((None, 8, w), lambda b, i: (b, jnp.minimum(i * r8 + r8, ntot // 8 - 1), 0)),
                  pl.BlockSpec(conv_w8.shape, lambda b, i: (0, 0)),
                  pl.BlockSpec(ones_a.shape, lambda b, i: (0, 0))],
        out_specs=[row(A_QK_W), row(A_QK_W), row(A_W)],
        out_shape=[jax.ShapeDtypeStruct((nb, ntot, A_QK_W), F32)] * 3,
        scratch_shapes=[pltpu.VMEM((TM + 16, w), F32)],
        compiler_params=_params(("parallel", "arbitrary")),
        name="dn_prep",
    )(zqkv, zqkv, zqkv, conv_w8, ones_a)


def _dn_pre(d, rows, q_ref, k_ref, v_ref, ab_ref, alog, dtb, eg_ref, eb_ref):
    c = DN_CHUNK
    q = q_ref[rows, :]
    k = k_ref[rows, :]
    v = v_ref[rows, :]
    ab = ab_ref[rows, :]
    g = -jnp.exp(alog) * _softplus(ab + dtb)
    beta = jax.nn.sigmoid(ab)
    ri = lax.broadcasted_iota(jnp.int32, (c, c), 0)
    ci = lax.broadcasted_iota(jnp.int32, (c, c), 1)
    if d == 0:
        incl, strict = ri >= ci, ri > ci
    else:
        incl, strict = ri <= ci, ri < ci
    cum = jnp.where(incl, 1.0, 0.0).astype(BF16)
    gc = _dot_exact_lhs(cum, g)
    last = c - 1 if d == 0 else 0
    g_last = gc[last:last + 1, :]
    egc = jnp.exp(gc)
    ekd = jnp.exp(g_last - gc)
    egl = jnp.broadcast_to(jnp.exp(g_last), (8, LANE))
    eg = eg_ref[d]
    eb = eb_ref[d]
    beta_x = _dot_exact_rhs(beta, eb)
    gx = _dot_exact_rhs(jnp.concatenate([egc, ekd, gc, egl], axis=0), eg)
    egc_x, ekd_x, gc_x, egl_x = gx[0:c], gx[c:2 * c], gx[2 * c:3 * c], gx[3 * c:3 * c + 1]
    gc_t = gc.T
    kbeta = k * beta_x
    vbeta = v * beta_x
    wrhs = kbeta * egc_x
    qd = q * egc_x
    kd = k * ekd_x
    lane = lax.broadcasted_iota(jnp.int32, (1, LANE), 1)
    lo = lane < DN_DK
    pairs = []
    for p in range(DN_HEADS // 2):
        sl = slice(LANE * p, LANE * p + LANE)
        k_s, q_s = k[:, sl], q[:, sl]
        kb_s, vb_s, wr_s = kbeta[:, sl], vbeta[:, sl], wrhs[:, sl]
        gcx_s = gc_x[:, sl]
        k_sb = k_s.astype(BF16)
        mats, rhss, attns = [], [], []
        for j in range(2):
            h = 2 * p + j
            mine = lo if j == 0 else jnp.logical_not(lo)
            gcol = (gcx_s if j == 0 else pltpu.roll(gcx_s, DN_DK, 1))[:, :c]
            grow = gc_t[DN_HEADS * d + h:DN_HEADS * d + h + 1, :]
            diff = gcol - grow
            dec = jnp.where(incl, jnp.exp(jnp.where(incl, diff, 0.0)), 0.0)
            kk = _dot_nt(jnp.where(mine, kb_s, 0.0).astype(BF16), k_sb)
            mats.append(jnp.where(strict, kk * dec, 0.0))
            qk = _dot_nt(jnp.where(mine, q_s, 0.0).astype(BF16), k_sb)
            attns.append((qk * dec).astype(BF16))
            if j == 0:
                rhss.append(jnp.where(lo, vb_s, pltpu.roll(wr_s, DN_DK, 1)))
            else:
                rhss.append(jnp.where(lo, pltpu.roll(vb_s, DN_DK, 1), wr_s))
        pairs.append(dict(mats=mats, rhss=rhss, attns=attns, qd=qd[:, sl].astype(BF16),
                          kd_t=kd[:, sl].T.astype(BF16), egl=egl_x[:, sl]))
    return pairs


def _dn_solve(mats, rhss):
    c = DN_CHUNK
    pw, xs = list(mats), list(rhss)
    rounds = 6
    for r in range(rounds):
        for n in range(len(pw)):
            if r < rounds - 1:
                both = _dot3(pw[n], jnp.concatenate([xs[n], pw[n]], axis=1))
                px, pw[n] = both[:, :2 * c], both[:, 2 * c:]
            else:
                px = _dot3(pw[n], xs[n])
            xs[n] = xs[n] - px if r == 0 else xs[n] + px
    return xs


def _dn_scan_kernel(alog_ref, dtb_ref, eg_ref, eb_ref,
                    qf, kf, vf, abf, qb, kb, vb, abb, of_ref, ob_ref, s_ref):
    i = pl.program_id(1)

    @pl.when(i == 0)
    def _():
        s_ref[...] = jnp.zeros_like(s_ref)

    alog = alog_ref[...]
    dtb = dtb_ref[...]
    nch = TM // DN_CHUNK
    npair = DN_HEADS // 2
    lane = lax.broadcasted_iota(jnp.int32, (1, LANE), 1)
    lo = lane < DN_DK
    ri2 = lax.broadcasted_iota(jnp.int32, (LANE, LANE), 0)
    ci2 = lax.broadcasted_iota(jnp.int32, (LANE, LANE), 1)
    bdiag = (ri2 < DN_DK) == (ci2 < DN_DK)

    def body(cidx, carry):
        rf = pl.ds(pl.multiple_of(cidx * DN_CHUNK, DN_CHUNK), DN_CHUNK)
        rb = pl.ds(pl.multiple_of((nch - 1 - cidx) * DN_CHUNK, DN_CHUNK), DN_CHUNK)
        pairs = (_dn_pre(0, rf, qf, kf, vf, abf, alog, dtb, eg_ref, eb_ref)
                 + _dn_pre(1, rb, qb, kb, vb, abb, alog, dtb, eg_ref, eb_ref))
        xs = _dn_solve([m for pr in pairs for m in pr["mats"]], [r for pr in pairs for r in pr["rhss"]])
        for n, pr in enumerate(pairs):
            d, p = divmod(n, npair)
            x0, x1 = xs[2 * n], xs[2 * n + 1]
            u = jnp.where(lo, x0, pltpu.roll(x1, DN_DK, 1))
            w = jnp.where(lo, pltpu.roll(x0, DN_DK, 1), x1)
            s = s_ref[n]
            sb = s.astype(BF16)
            v_new = u - _dot(w.astype(BF16), sb)
            vn_b = v_new.astype(BF16)
            o = _dot(pr["qd"], sb)
            o = o + _dot(pr["attns"][0], jnp.where(lo, vn_b, jnp.zeros_like(vn_b)))
            o = o + _dot(pr["attns"][1], jnp.where(lo, jnp.zeros_like(vn_b), vn_b))
            upd = _dot(pr["kd_t"], vn_b)
            s_ref[n] = s * pr["egl"] + jnp.where(bdiag, upd, 0.0)
            o_ref, rows = (of_ref, rf) if d == 0 else (ob_ref, rb)
            o_ref[rows, LANE * p:LANE * p + LANE] = o
        return carry

    lax.fori_loop(0, nch, body, 0)


def _dn_scan(q, k, v, ab, alog, dtb, eg, eb):
    nb, ntot, w = q.shape
    nt = ntot // TM
    nl = nt - 1
    fwd = lambda b, i: (b, jnp.where(i == 0, nl, i - 1), 0)
    bwd = lambda b, i: (b, jnp.where(i == 0, nl, nl - i), 0)
    const = lambda a: pl.BlockSpec(a.shape, lambda b, i: (0,) * a.ndim)
    blk = lambda ww, im: pl.BlockSpec((None, TM, ww), im)
    return pl.pallas_call(
        _dn_scan_kernel,
        grid=(nb, nt),
        in_specs=[const(alog), const(dtb), const(eg), const(eb),
                  blk(w, fwd), blk(w, fwd), blk(w, fwd), blk(LANE, fwd),
                  blk(w, bwd), blk(w, bwd), blk(w, bwd), blk(LANE, bwd)],
        out_specs=[blk(w, fwd), blk(w, bwd)],
        out_shape=[jax.ShapeDtypeStruct((nb, ntot, w), F32)] * 2,
        scratch_shapes=[pltpu.VMEM((2 * (DN_HEADS // 2), LANE, LANE), F32)],
        compiler_params=_params(("parallel", "arbitrary")),
        name="dn_scan",
    )(alog, dtb, eg, eb, q, k, v, ab, q, k, v, ab)


def _diff_attn_kernel(lam_init, q_ref, k_ref, vt_ref, lam_ref, g_ref, ones_ref, o_ref, st_ref):
    i = pl.program_id(1)
    nl = pl.num_programs(1) - 1
    lv = lam_ref[...]
    lam = (jnp.exp(jnp.sum(lv[0:1] * lv[1:2], axis=-1, keepdims=True))
           - jnp.exp(jnp.sum(lv[2:3] * lv[3:4], axis=-1, keepdims=True)) + lam_init)
    lane = lax.broadcasted_iota(jnp.int32, (1, B_QK_W), 1)

    nsm = 2 * DA_HEADS

    def run(tiles):
        q = q_ref[...]
        ones16 = jnp.ones((16, TM), BF16)
        m_prev = None
        res = []
        for n in range(nsm + 1):
            if n < nsm:
                slot = slice(DA_QK * n // LANE * LANE, DA_QK * n // LANE * LANE + LANE)
                lo = DA_QK * n
                qm = jnp.where(jnp.logical_and(lane >= lo, lane < lo + DA_QK), q, jnp.zeros_like(q))[:, slot]
            hp = (n - 1) // 2
            m8 = jnp.full((8, TM), -jnp.inf, F32)
            acc = jnp.zeros((DA_V + 16, TM), F32)
            for c in tiles:
                rows = slice(c * TM, c * TM + TM)
                if n < nsm:
                    st = _dot_nt(k_ref[rows, slot], qm)
                    st_ref[n % 2, rows, :] = st
                    m8 = jnp.maximum(m8, jnp.max(st.reshape(TM // 8, 8, TM), axis=0))
                if n > 0:
                    e = jnp.exp2(st_ref[(n - 1) % 2, rows, :] - m_prev).astype(BF16)
                    lhs = jnp.concatenate([vt_ref[c, DA_V * hp:DA_V * hp + DA_V, :], ones16], axis=0)
                    acc = acc + _dot(lhs, e)
            if n > 0:
                res.append(acc[:DA_V] / acc[DA_V:DA_V + 1])
            if n < nsm:
                m_prev = jnp.max(m8, axis=0, keepdims=True)
        ot = jnp.concatenate([res[2 * h] - lam * res[2 * h + 1] for h in range(DA_HEADS)], axis=0)
        ms = _dot_exact_lhs(ones_ref[...], ot * ot, parts=2)
        yt = (ot * lax.rsqrt(ms + EPS)) * g_ref[...]
        o_ref[...] = (yt * (1.0 - lam_init)).T.astype(BF16)

    ntiles = st_ref.shape[1] // TM

    @pl.when(i < nl)
    def _():
        run(range(ntiles))

    @pl.when(i == nl)
    def _():
        run([ntiles - 1])


def _diff_attn(q, k, vt, lam_vecs, subln_g, ones_b, lam_init):
    nb, ntot, w = q.shape
    nt = ntot // TM
    row = pl.BlockSpec((None, TM, w), lambda b, i: (b, i, 0))
    full = pl.BlockSpec((None, ntot, w), lambda b, i: (b, 0, 0))
    full_t = pl.BlockSpec((None,) + vt.shape[1:], lambda b, i: (b, 0, 0, 0))
    const = lambda a: pl.BlockSpec(a.shape, lambda b, i: (0,) * a.ndim)
    return pl.pallas_call(
        functools.partial(_diff_attn_kernel, lam_init),
        grid=(nb, nt),
        in_specs=[row, full, full_t, const(lam_vecs), const(subln_g), const(ones_b)],
        out_specs=row,
        out_shape=jax.ShapeDtypeStruct((nb, ntot, w), BF16),
        scratch_shapes=[pltpu.VMEM((2, ntot, TM), F32)],
        compiler_params=_params(("parallel", "arbitrary")),
        name="diff_attn",
    )(q, k, vt, lam_vecs, subln_g, ones_b)


def _win_attn_kernel(n_lat, sink_ref, q_ref, k_ref, v_ref, o_ref):
    i = pl.program_id(1)
    nl = pl.num_programs(1) - 1
    rep = WA_HEADS // WA_KV_HEADS
    lane = lax.broadcasted_iota(jnp.int32, (1, LANE), 1)
    lo = lane < WA_DIM
    rowg = lax.broadcasted_iota(jnp.int32, (rep * TM, 1), 0) // TM

    def run(k_all, v_all, bias):
        outs = []
        for g in range(WA_KV_HEADS):
            mine = lo if g == 0 else jnp.logical_not(lo)
            q3 = jnp.concatenate(
                [jnp.where(mine, q_ref[:, LANE * s:LANE * s + LANE], jnp.zeros((TM, LANE), BF16))
                 for s in range(rep)], axis=0)
            s = _dot_nt(q3, k_all)
            if bias is not None:
                s = s + bias
            sk = jnp.zeros((rep * TM, 1), F32)
            for r in range(rep):
                sk = jnp.where(rowg == r, sink_ref[rep * g + r], sk)
            m = jnp.maximum(jnp.max(s, axis=-1, keepdims=True), sk)
            e = jnp.exp(s - m)
            den = jnp.sum(e, axis=-1, keepdims=True) + jnp.exp(sk - m)
            outs.append(_dot(e.astype(BF16), v_all) / den)
        for s in range(rep):
            o_ref[:, LANE * s:LANE * s + LANE] = jnp.where(
                lo, outs[0][TM * s:TM * s + TM], outs[1][TM * s:TM * s + TM]).astype(BF16)

    kc = k_ref[n_lat:, :]
    vc = v_ref[n_lat:, :]
    band = 3 * TM

    @pl.when(i < nl)
    def _():
        start = pl.multiple_of(jnp.clip((i - 1) * TM, 0, n_lat - band), TM)
        kb = k_ref[pl.ds(start, band), :]
        vb = v_ref[pl.ds(start, band), :]
        qpos = i * TM + lax.broadcasted_iota(jnp.int32, (TM, 1), 0)
        kpos = start + lax.broadcasted_iota(jnp.int32, (1, band), 1)
        near = jnp.where(jnp.abs(qpos - kpos) <= WINDOW, 0.0, -1e30)
        bias = jnp.concatenate([near, jnp.zeros((TM, kc.shape[0]), F32)], axis=1)
        bias = jnp.concatenate([bias] * rep, axis=0)
        run(jnp.concatenate([kb, kc], axis=0), jnp.concatenate([vb, vc], axis=0), bias)

    @pl.when(i == nl)
    def _():
        run(kc, vc, None)


def _win_attn(q, k, v, sink, n_lat):
    nb, ntot, w = q.shape
    nt = ntot // TM
    kw = k.shape[-1]
    row = pl.BlockSpec((None, TM, w), lambda b, i: (b, i, 0))
    full = pl.BlockSpec((None, ntot, kw), lambda b, i: (b, 0, 0))
    return pl.pallas_call(
        functools.partial(_win_attn_kernel, n_lat),
        grid=(nb, nt),
        in_specs=[pl.BlockSpec(memory_space=pltpu.SMEM), row, full, full],
        out_specs=row,
        out_shape=jax.ShapeDtypeStruct((nb, ntot, w), BF16),
        compiler_params=_params(("parallel", "arbitrary")),
        name="win_attn",
    )(sink, q, k, v)


def _out_proj_kernel(x_ref, of_ref, ob_ref, gate_ref, yb_ref, yw_ref, wa_ref, wb_ref, wc_ref, dng_ref,
                     ones_ref, g1_ref, sc2_ref, sh2_ref, n2_ref, wr_ref, br_ref, xn_o, h2_o, lg_o):
    o = of_ref[...] + ob_ref[...]
    ms = _dot_exact_rhs(o * o, ones_ref[...], parts=2)
    ya = ((o * lax.rsqrt(ms + EPS)) * dng_ref[...]) * _silu(gate_ref[...])
    y = _dot(ya.astype(BF16), wa_ref[...]) + _dot(yb_ref[...], wb_ref[...]) + _dot(yw_ref[...], wc_ref[...])
    xn = x_ref[...] + g1_ref[...] * y
    xn_o[...] = xn
    h2 = _rms_mod(xn, n2_ref[...], sc2_ref[...], sh2_ref[...])
    h2_o[...] = h2.astype(BF16)
    lg_o[...] = _dot3(h2, wr_ref[...]) + br_ref[...]


def _out_proj(x, of, ob, gate, yb, yw, wa, wb, wc, dng, ones_a, mod, n2g, wr, br, nb, ntot):
    d = x.shape[-1]
    nt = ntot // TM
    nl = nt - 1
    row = lambda w: pl.BlockSpec((None, TM, w), lambda b, i: (b, i, 0))
    const = lambda a: pl.BlockSpec(a.shape, lambda b, i: (0,) * a.ndim)
    return pl.pallas_call(
        _out_proj_kernel,
        grid=(nb, nt),
        in_specs=[row(d), row(A_W), row(A_W), row(A_W), row(B_W), row(C_W),
                  const(wa), const(wb), const(wc), const(dng), const(ones_a),
                  _mod_spec_d(2, nb, nl, d), _mod_spec_d(4, nb, nl, d), _mod_spec_d(3, nb, nl, d),
                  const(n2g), const(wr), const(br)],
        out_specs=[row(d), row(d), row(LANE)],
        out_shape=[jax.ShapeDtypeStruct((nb, ntot, d), F32), jax.ShapeDtypeStruct((nb, ntot, d), BF16),
                   jax.ShapeDtypeStruct((nb, ntot, LANE), F32)],
        compiler_params=_params(("parallel", "arbitrary")),
        name="out_proj",
    )(x, of, ob, gate, yb, yw, wa, wb, wc, dng, ones_a, mod, mod, mod, n2g, wr, br)


def _route_kernel(lg_ref, gw_o, pos_o, cnt_o):
    t = lg_ref.shape[0]
    lg = lg_ref[...]
    lane_i = lax.broadcasted_iota(jnp.int32, (1, LANE), 1)
    lane = lane_i.astype(F32)
    big = float(LANE)
    neg = -jnp.inf
    is_g = lane_i < N_GROUPS
    lgm = jnp.where(is_g, lg, neg)
    mg = jnp.max(lgm, axis=-1, keepdims=True)
    p_sel = 1.0 / jnp.sum(jnp.where(is_g, jnp.exp(lgm - mg), 0.0), axis=-1, keepdims=True)
    gidx = jnp.min(jnp.where(jnp.logical_and(is_g, lgm == mg), lane, big), axis=-1, keepdims=True)
    e_lane = lane_i - N_GROUPS
    in_grp = jnp.logical_and(jnp.logical_and(e_lane >= 0, e_lane < N_EXPERTS),
                             jnp.floor((lane - N_GROUPS) * (1.0 / EXP_PER_GROUP)) == gidx)
    le = jnp.where(in_grp, lg, neg)
    v1 = jnp.max(le, axis=-1, keepdims=True)
    i1 = jnp.min(jnp.where(jnp.logical_and(in_grp, le == v1), lane, big), axis=-1, keepdims=True)
    is1 = lane == i1
    le2 = jnp.where(is1, neg, le)
    v2 = jnp.max(le2, axis=-1, keepdims=True)
    rest = jnp.logical_and(in_grp, jnp.logical_not(is1))
    i2 = jnp.min(jnp.where(jnp.logical_and(rest, le2 == v2), lane, big), axis=-1, keepdims=True)
    is2 = lane == i2
    e2 = jnp.exp(v2 - v1)
    w1 = 1.0 / (1.0 + e2)
    w2 = e2 / (1.0 + e2)
    gw = jnp.where(is1, p_sel * w1, jnp.where(is2, p_sel * w2, 0.0))
    sel = jnp.logical_or(is1, is2)
    self_ = jnp.where(sel, 1.0, 0.0)
    ri = lax.broadcasted_iota(jnp.int32, (ROUTE_BLK, ROUTE_BLK), 0)
    ci = lax.broadcasted_iota(jnp.int32, (ROUTE_BLK, ROUTE_BLK), 1)
    tri = jnp.where(ri > ci, 1.0, 0.0).astype(BF16)
    run = jnp.zeros((1, LANE), F32)
    pos_blocks = []
    for b in range(t // ROUTE_BLK):
        blk = self_[b * ROUTE_BLK:(b + 1) * ROUTE_BLK]
        pos_blocks.append(_dot(tri, blk.astype(BF16)) + run)
        run = run + jnp.sum(blk, axis=0, keepdims=True)
    pos = jnp.where(sel, jnp.concatenate(pos_blocks, axis=0), -1.0)
    gw_o[...] = gw.T
    pos_o[...] = pos.T
    cnt_o[...] = jnp.broadcast_to(run, (8, LANE)).astype(jnp.int32)


def _route(logits):
    ntok = logits.shape[0]
    ntile = ntok // MOE_T
    return pl.pallas_call(
        _route_kernel,
        grid=(ntile,),
        in_specs=[pl.BlockSpec((MOE_T, LANE), lambda t: (t, 0))],
        out_specs=[pl.BlockSpec((None, LANE, MOE_T), lambda t: (t, 0, 0)),
                   pl.BlockSpec((None, LANE, MOE_T), lambda t: (t, 0, 0)),
                   pl.BlockSpec((None, 8, LANE), lambda t: (t, 0, 0))],
        out_shape=[jax.ShapeDtypeStruct((ntile, LANE, MOE_T), F32),
                   jax.ShapeDtypeStruct((ntile, LANE, MOE_T), F32),
                   jax.ShapeDtypeStruct((ntile, 8, LANE), jnp.int32)],
        compiler_params=_params(("parallel",)),
        name="moe_route",
    )(logits)


def _moe_kernel(cnt_ref, h_ref, pos_ref, gw_ref, wg_ref, wu_ref, wd_ref, o_ref):
    t = pl.program_id(0)
    e = pl.program_id(1)

    @pl.when(e == 0)
    def _():
        o_ref[...] = jnp.zeros_like(o_ref)

    n = cnt_ref[t * N_EXPERTS + e]
    prow = pos_ref[pl.ds(N_GROUPS + e, 1), :]
    grow = gw_ref[pl.ds(N_GROUPS + e, 1), :]

    def chunk(c, carry):
        slot = (lax.broadcasted_iota(jnp.int32, (MOE_CH, 1), 0) + c * MOE_CH).astype(F32)
        hit = prow == slot
        onehot = jnp.where(hit, 1.0, 0.0).astype(BF16)
        hc = _dot(onehot, h_ref[...]).astype(BF16)
        a = _dot(hc, wg_ref[...])
        b = _dot(hc, wu_ref[...])
        y = _dot((_silu(a) * b).astype(BF16), wd_ref[...])
        gcol = jnp.sum(jnp.where(hit, grow, 0.0), axis=-1, keepdims=True)
        o_ref[...] += _dot_tn(onehot, (y * gcol).astype(BF16))
        return carry

    lax.fori_loop(0, (n + MOE_CH - 1) // MOE_CH, chunk, 0)


def _moe(counts, h2, pos_t, gw_t, wg, wu, wd):
    ntok, d = h2.shape
    ntile = ntok // MOE_T
    gs = pltpu.PrefetchScalarGridSpec(
        num_scalar_prefetch=1,
        grid=(ntile, N_EXPERTS),
        in_specs=[pl.BlockSpec((MOE_T, d), lambda t, e, c: (t, 0)),
                  pl.BlockSpec((None, LANE, MOE_T), lambda t, e, c: (t, 0, 0)),
                  pl.BlockSpec((None, LANE, MOE_T), lambda t, e, c: (t, 0, 0)),
                  pl.BlockSpec((None, d, D_EXPERT), lambda t, e, c: (e, 0, 0)),
                  pl.BlockSpec((None, d, D_EXPERT), lambda t, e, c: (e, 0, 0)),
                  pl.BlockSpec((None, D_EXPERT, d), lambda t, e, c: (e, 0, 0))],
        out_specs=pl.BlockSpec((MOE_T, d), lambda t, e, c: (t, 0)),
    )
    return pl.pallas_call(
        _moe_kernel,
        grid_spec=gs,
        out_shape=jax.ShapeDtypeStruct((ntok, d), F32),
        compiler_params=_params(("parallel", "arbitrary")),
        name="moe_experts",
    )(counts, h2, pos_t, gw_t, wg, wu, wd)


def _final_kernel(xn_ref, ff_ref, g2_ref, g_ref, o_ref):
    x = xn_ref[...] + g2_ref[...] * ff_ref[...]
    y = x * lax.rsqrt(jnp.mean(x * x, axis=-1, keepdims=True) + EPS)
    o_ref[...] = y * g_ref[...]


def _final(xn, ff, mod, g, nb, n_lat, ntot):
    d = xn.shape[-1]
    nl = n_lat // TM
    row = pl.BlockSpec((None, TM, d), lambda b, i: (b, i, 0))
    return pl.pallas_call(
        _final_kernel,
        grid=(nb, nl),
        in_specs=[row, row, _mod_spec_d(5, nb, nl, d), pl.BlockSpec(g.shape, lambda b, i: (0, 0))],
        out_specs=row,
        out_shape=jax.ShapeDtypeStruct((nb, n_lat, d), F32),
        compiler_params=_params(("parallel", "arbitrary")),
        name="final_norm",
    )(xn, ff, mod, g)


def _rope_swap_perm(width, dim):
    nf = dim // 4
    j = jnp.arange(width)
    base = (j // (2 * nf)) * (2 * nf)
    return base + (j % (2 * nf) + nf) % (2 * nf)


def _rope_tables(n_lat, n_ctx, dim, width):
    nf = dim // 4
    t = jnp.arange(n_lat)
    row = (t // GRID_W).astype(F32)
    col = (t % GRID_W).astype(F32)
    inv = ROPE_BASE ** (-jnp.arange(nf, dtype=F32) / nf)
    j = jnp.arange(width) % dim
    axis = j // (2 * nf)
    pos = jnp.where(axis[None, :] == 0, row[:, None], col[:, None])
    ang = pos * inv[j % nf][None, :]
    sign = jnp.where(j % (2 * nf) < nf, -1.0, 1.0).astype(F32)
    cos = jnp.concatenate([jnp.cos(ang), jnp.ones((n_ctx, width), F32)], axis=0)
    sin = jnp.concatenate([jnp.sin(ang) * sign[None, :], jnp.zeros((n_ctx, width), F32)], axis=0)
    return cos, sin


def _block_ones(width, group, value):
    j = jnp.arange(width)
    return jnp.where((j[:, None] // group) == (j[None, :] // group), value, 0.0).astype(BF16)


def _slot_cols():
    rep = WA_HEADS // WA_KV_HEADS
    cols = []
    for s in range(rep):
        cols.append(jnp.arange(WA_DIM) + WA_DIM * s)
        cols.append(jnp.arange(WA_DIM) + WA_DIM * (rep + s))
    return jnp.concatenate(cols)


def _build_w_in(w):
    d = w.shape[0]
    s1, s2 = A_COLS, A_COLS + B_COLS
    wa, wb, wc = w[:, :s1], w[:, s1:s2], w[:, s2:]
    ab = jnp.pad(wa[:, QKV_W + A_W:], ((0, 0), (0, LANE - 4 * DN_HEADS)))
    bq, bk, bv = wb[:, :B_QK_W], wb[:, B_QK_W:2 * B_QK_W], wb[:, 2 * B_QK_W:]
    pb = _rope_swap_perm(B_QK_W, DA_QK)
    cq = wc[:, :C_W][:, _slot_cols()]
    ck, cv = wc[:, C_W:C_W + C_KV_W], wc[:, C_W + C_KV_W:]
    pcq = _rope_swap_perm(C_W, WA_DIM)
    pck = _rope_swap_perm(C_KV_W, WA_DIM)
    cat = jnp.concatenate([wa[:, :QKV_W], wa[:, QKV_W:QKV_W + A_W], ab,
                           bq, bk, bv, bq[:, pb], bk[:, pb],
                           cq, ck, cv, cq[:, pcq], ck[:, pck]], axis=1)
    assert cat.shape == (d, _O_END)
    return cat.astype(BF16)


def _expand_mats():
    r = jnp.arange(LANE)[:, None]
    h = (jnp.arange(A_QK_W) // DN_DK)[None, :]
    eg = jnp.stack([(r == DN_HEADS * d + h) for d in range(2)]).astype(BF16)
    eb = jnp.stack([(r == 2 * DN_HEADS + DN_HEADS * d + h) for d in range(2)]).astype(BF16)
    return eg, eb


def kernel(x, c, ctx, c_ctx, ada_w, ada_b, norm1_g, norm2_g, w_in, dn_conv_w, dn_a_log, dn_dt_bias, dn_norm_g, da_lambda, da_subln_g, wa_sink, w_out, router_group_w, router_group_b, router_expert_w, router_expert_b, exp_w_gate, exp_w_up, exp_w_down, final_norm_g):
    nb, n_lat, d = x.shape
    n_ctx = ctx.shape[1]
    depth = ada_w.shape[0]
    assert n_ctx == TM and n_lat % TM == 0 and n_lat >= 3 * TM
    ntot = n_lat + n_ctx
    ntok = nb * ntot
    assert ntok % MOE_T == 0

    xa = jnp.concatenate([x, ctx], axis=1)
    cc = jnp.zeros((16, d), F32).at[:nb].set(c).at[nb].set(c_ctx)
    cosb, sinb = _rope_tables(n_lat, n_ctx, DA_QK, B_QK_W)
    cosc, sinc = _rope_tables(n_lat, n_ctx, WA_DIM, C_KV_W)
    tabs = (cosb, sinb, cosc, sinc)
    ones_a = _block_ones(A_W, DN_DV, 1.0)
    mean_a = _block_ones(A_W, DN_DV, 1.0 / DN_DV)
    mean_b = _block_ones(B_W, DA_V, 1.0 / DA_V)
    eg, eb = _expand_mats()
    slot_rows = _slot_cols()
    pad_lane = lambda v: jnp.pad(v.reshape(1, -1), ((0, 0), (0, LANE - v.size)))

    mods = [_ada(cc, ada_w[li], ada_b[li]).reshape(16, 6, 1, d) for li in range(depth)]
    xn = ff = None
    for li in range(depth):
        mod = mods[li]
        w_cat = _build_w_in(w_in[li])
        n1g = norm1_g[li].reshape(1, d)
        if li == 0:
            outs = _in_proj((xa,), (mod,), n1g, w_cat, tabs, nb, ntot)
            xcur = xa
        else:
            outs = _in_proj((xn, ff.reshape(nb, ntot, d)), (mod, mods[li - 1]), n1g, w_cat, tabs, nb, ntot)
            xcur, outs = outs[0], outs[1:]
        zqkv, gate, ab, qb, kb, vb, qc, kc, vc = outs

        conv_w8 = jnp.pad(dn_conv_w[li], ((0, 8 - DN_CONV), (0, 0)))
        q, k, v = _dn_prep(zqkv, conv_w8, ones_a)
        of, ob = _dn_scan(q, k, v, ab, pad_lane(dn_a_log[li]), pad_lane(dn_dt_bias[li]), eg, eb)

        lam_init = 0.8 - 0.6 * math.exp(-0.3 * li)
        yb = _diff_attn(qb, kb, vb, da_lambda[li], jnp.tile(da_subln_g[li], DA_HEADS).reshape(B_W, 1),
                        mean_b, lam_init)
        yw = _win_attn(qc, kc, vc, jnp.pad(wa_sink[li], (0, 8 - WA_HEADS)), n_lat)

        wo = w_out[li]
        wa_o = wo[:A_W].astype(BF16)
        wb_o = wo[A_W:A_W + B_W].astype(BF16)
        wc_o = wo[A_W + B_W:][slot_rows].astype(BF16)
        wr = jnp.pad(jnp.concatenate([router_group_w[li], router_expert_w[li]], axis=1),
                     ((0, 0), (0, LANE - N_GROUPS - N_EXPERTS)))
        br = pad_lane(jnp.concatenate([router_group_b[li], router_expert_b[li]]))
        xn, h2, logits = _out_proj(xcur, of, ob, gate, yb, yw, wa_o, wb_o, wc_o,
                                   jnp.tile(dn_norm_g[li], DN_HEADS).reshape(1, A_W), mean_a, mod,
                                   norm2_g[li].reshape(1, d), wr, br, nb, ntot)

        gw_t, pos_t, cnt = _route(logits.reshape(ntok, LANE))
        counts = cnt[:, 0, N_GROUPS:N_GROUPS + N_EXPERTS].reshape(-1)
        ff = _moe(counts, h2.reshape(ntok, d), pos_t, gw_t, exp_w_gate[li].astype(BF16),
                  exp_w_up[li].astype(BF16), exp_w_down[li].astype(BF16))

    return _final(xn, ff.reshape(nb, ntot, d), mods[depth - 1], final_norm_g.reshape(1, d), nb, n_lat, ntot)
```

```python
import functools
import math

import jax
import jax.numpy as jnp
from jax import lax
from jax.experimental import pallas as pl
from jax.experimental.pallas import tpu as pltpu

F32 = jnp.float32
BF16 = jnp.bfloat16

GRID_W = 64
EPS = 1e-6
LOG2E = math.log2(math.e)
ROPE_BASE = 10000.0
DN_HEADS = 6
DN_DK = 64
DN_DV = 64
DN_CONV = 5
DN_CHUNK = 64
DA_HEADS = 4
DA_QK = 32
DA_V = 64
WA_HEADS = 6
WA_KV_HEADS = 2
WA_DIM = 64
WINDOW = 128
N_GROUPS = 4
EXP_PER_GROUP = 4
N_EXPERTS = 16
D_EXPERT = 512

A_QK_W = DN_HEADS * DN_DK
A_W = DN_HEADS * DN_DV
QKV_W = 2 * A_QK_W + A_W
B_W = DA_HEADS * DA_V
B_QK_W = 2 * DA_HEADS * DA_QK
C_W = WA_HEADS * WA_DIM
C_KV_W = WA_KV_HEADS * WA_DIM
A_COLS = QKV_W + A_W + 4 * DN_HEADS
B_COLS = 2 * B_QK_W + B_W

LANE = 128
TM = 256
MOE_T = 1024
MOE_CH = 192
ROUTE_BLK = 256
VMEM_LIMIT = 56 * 1024 * 1024

_O_QKV = 0
_O_GATE = _O_QKV + QKV_W
_O_AB = _O_GATE + A_W
_O_BQ = _O_AB + LANE
_O_BK = _O_BQ + B_QK_W
_O_BV = _O_BK + B_QK_W
_O_BQS = _O_BV + B_W
_O_BKS = _O_BQS + B_QK_W
_O_CQ = _O_BKS + B_QK_W
_O_CK = _O_CQ + C_W
_O_CV = _O_CK + C_KV_W
_O_CQS = _O_CV + C_KV_W
_O_CKS = _O_CQS + C_W
_O_END = _O_CKS + C_KV_W


def _dot(a, b):
    return jnp.dot(a, b, preferred_element_type=F32)


def _dot_nt(a, b):
    return lax.dot_general(a, b, (((1,), (1,)), ((), ())), preferred_element_type=F32)


def _dot_tn(a, b):
    return lax.dot_general(a, b, (((0,), (0,)), ((), ())), preferred_element_type=F32)


def _split2(a):
    hi = a.astype(BF16)
    lo = (a - hi.astype(F32)).astype(BF16)
    return hi, lo


def _split3(a):
    hi = a.astype(BF16)
    r = a - hi.astype(F32)
    mid = r.astype(BF16)
    lo = (r - mid.astype(F32)).astype(BF16)
    return hi, mid, lo


def _dot3(a, b):
    ah, al = _split2(a)
    bh, bl = _split2(b)
    return _dot(ah, bh) + (_dot(ah, bl) + _dot(al, bh))


def _dot_exact_rhs(a, b_bf16, parts=3):
    sp = _split3(a) if parts == 3 else _split2(a)
    out = _dot(sp[0], b_bf16)
    for p in sp[1:]:
        out = out + _dot(p, b_bf16)
    return out


def _dot_exact_lhs(a_bf16, b, parts=3):
    sp = _split3(b) if parts == 3 else _split2(b)
    out = _dot(a_bf16, sp[0])
    for p in sp[1:]:
        out = out + _dot(a_bf16, p)
    return out


def _col_reduce(x, op, slab=64):
    n, w = x.shape
    if n > slab and n % slab == 0:
        x = op(x.reshape(n // slab, slab, w), axis=0)
    return op(x, axis=0, keepdims=True)


def _silu(x):
    return x * jax.nn.sigmoid(x)


def _softplus(x):
    return jnp.maximum(x, 0.0) + jnp.log1p(jnp.exp(-jnp.abs(x)))


def _params(sem):
    return pltpu.CompilerParams(dimension_semantics=sem, vmem_limit_bytes=VMEM_LIMIT)


def _ada_kernel(c_ref, w_ref, b_ref, o_ref):
    o_ref[...] = _dot3(_silu(c_ref[...]), w_ref[...]) + b_ref[...]


def _ada(cc, w, b):
    rows, d = cc.shape
    n = w.shape[1]
    tn = n // 4
    return pl.pallas_call(
        _ada_kernel,
        grid=(n // tn,),
        in_specs=[pl.BlockSpec((rows, d), lambda j: (0, 0)),
                  pl.BlockSpec((d, tn), lambda j: (0, j)),
                  pl.BlockSpec((1, tn), lambda j: (0, j))],
        out_specs=pl.BlockSpec((rows, tn), lambda j: (0, j)),
        out_shape=jax.ShapeDtypeStruct((rows, n), F32),
        compiler_params=_params(("arbitrary",)),
        name="ada_mod",
    )(cc, w, b.reshape(1, n))


def _mod_spec_d(k, nb, nl, d):
    return pl.BlockSpec((None, None, 1, d), lambda b, i: (jnp.where(i == nl, nb, b), k, 0, 0))


def _rms_mod(x, g, sc, sh):
    y = x * lax.rsqrt(jnp.mean(x * x, axis=-1, keepdims=True) + EPS)
    return (y * g) * (1.0 + sc) + sh


def _in_proj_kernel(fuse_res, *refs):
    if fuse_res:
        xn_ref, ff_ref, g2_ref = refs[:3]
        refs = refs[3:]
    else:
        x_ref = refs[0]
        refs = refs[1:]
    (sc_ref, sh_ref, g_ref, w_ref, cosb_ref, sinb_ref, cosc_ref, sinc_ref) = refs[:8]
    outs = refs[8:]
    if fuse_res:
        x = xn_ref[...] + g2_ref[...] * ff_ref[...]
        outs[0][...] = x
        outs = outs[1:]
    else:
        x = x_ref[...]
    (zqkv_o, gate_o, ab_o, qb_o, kb_o, vb_o, qc_o, kc_o, vc_o) = outs
    hb = _rms_mod(x, g_ref[...], sc_ref[...], sh_ref[...]).astype(BF16)

    def seg(a, b):
        return _dot(hb, w_ref[:, a:b])

    zqkv_o[...] = seg(_O_QKV, _O_GATE)
    gate_o[...] = seg(_O_GATE, _O_AB)
    ab_o[...] = seg(_O_AB, _O_BQ)
    cb = cosb_ref[...]
    sb = sinb_ref[...]
    qb_o[...] = ((seg(_O_BQ, _O_BK) * cb + seg(_O_BQS, _O_BKS) * sb) * (DA_QK ** -0.5 * LOG2E)).astype(BF16)
    kb_o[...] = (seg(_O_BK, _O_BV) * cb + seg(_O_BKS, _O_CQ) * sb).astype(BF16)
    vb_o[...] = seg(_O_BV, _O_BQS).T.astype(BF16)
    cc = cosc_ref[...]
    sc_ = sinc_ref[...]
    cc3 = jnp.concatenate([cc, cc, cc], axis=1)
    sc3 = jnp.concatenate([sc_, sc_, sc_], axis=1)
    qc_o[...] = ((seg(_O_CQ, _O_CK) * cc3 + seg(_O_CQS, _O_CKS) * sc3) * (WA_DIM ** -0.5)).astype(BF16)
    kc_o[...] = (seg(_O_CK, _O_CV) * cc + seg(_O_CKS, _O_END) * sc_).astype(BF16)
    vc_o[...] = seg(_O_CV, _O_CQS).astype(BF16)


def _in_proj(x_parts, mod, norm_g, w_cat, tabs, nb, ntot):
    fuse_res = len(x_parts) == 2
    d = x_parts[0].shape[-1]
    nt = ntot // TM
    nl = nt - 1
    row = lambda w: pl.BlockSpec((None, TM, w), lambda b, i: (b, i, 0))
    tab = lambda w: pl.BlockSpec((TM, w), lambda b, i: (i, 0))
    const = lambda a: pl.BlockSpec(a.shape, lambda b, i: (0,) * a.ndim)
    if fuse_res:
        (xn, ff), prev_mod = x_parts, mod[1]
        ins = [xn, ff, prev_mod]
        in_specs = [row(d), row(d), _mod_spec_d(5, nb, nl, d)]
        cur_mod = mod[0]
    else:
        ins = [x_parts[0]]
        in_specs = [row(d)]
        cur_mod = mod[0]
    ins += [cur_mod, cur_mod, norm_g, w_cat, *tabs]
    in_specs += [_mod_spec_d(1, nb, nl, d), _mod_spec_d(0, nb, nl, d), const(norm_g), const(w_cat),
                 tab(B_QK_W), tab(B_QK_W), tab(C_KV_W), tab(C_KV_W)]
    widths = [(QKV_W, F32), (A_W, F32), (LANE, F32), (B_QK_W, BF16), (B_QK_W, BF16), (None, BF16),
              (C_W, BF16), (C_KV_W, BF16), (C_KV_W, BF16)]
    if fuse_res:
        widths = [(d, F32)] + widths
    return pl.pallas_call(
        functools.partial(_in_proj_kernel, fuse_res),
        grid=(nb, nt),
        in_specs=in_specs,
        out_specs=[pl.BlockSpec((None, None, B_W, TM), lambda b, i: (b, i, 0, 0)) if w is None else row(w)
                   for w, _ in widths],
        out_shape=[jax.ShapeDtypeStruct((nb, nt, B_W, TM) if w is None else (nb, ntot, w), dt)
                   for w, dt in widths],
        compiler_params=_params(("parallel", "arbitrary")),
        name="in_proj",
    )(*ins)


def _dn_prep_kernel(zc_ref, zp_ref, zn_ref, w_ref, ones_ref, q_o, k_o, v_o, ext_ref):
    i = pl.program_id(1)
    nl = pl.num_programs(1) - 1
    prev_ok = jnp.logical_and(i >= 1, i <= nl - 1)
    next_ok = i <= nl - 2
    ext_ref[0:8, :] = jnp.where(prev_ok, zp_ref[...], 0.0)
    ext_ref[8:8 + TM, :] = zc_ref[...]
    ext_ref[8 + TM:16 + TM, :] = jnp.where(next_ok, zn_ref[...], 0.0)
    half = DN_CONV // 2
    acc = w_ref[0:1, :] * ext_ref[8 - half:8 - half + TM, :]
    for j in range(1, DN_CONV):
        acc = acc + w_ref[j:j + 1, :] * ext_ref[8 - half + j:8 - half + j + TM, :]
    y = _silu(acc)
    ones = ones_ref[...]

    def l2n(t):
        ss = _dot_exact_rhs(t * t, ones, parts=2)
        return t * lax.rsqrt(ss + EPS)

    q_o[...] = l2n(y[:, :A_QK_W]) * (DN_DK ** -0.5)
    k_o[...] = l2n(y[:, A_QK_W:2 * A_QK_W])
    v_o[...] = y[:, 2 * A_QK_W:]


def _dn_prep(zqkv, conv_w8, ones_a):
    nb, ntot, w = zqkv.shape
    nt = ntot // TM
    r8 = TM // 8
    row = lambda ww: pl.BlockSpec((None, TM, ww), lambda b, i: (b, i, 0))
    return pl.pallas_call(
        _dn_prep_kernel,
        grid=(nb, nt),
        in_specs=[row(w),
                  pl.BlockSpec((None, 8, w), lambda b, i: (b, jnp.maximum(i * r8 - 1, 0), 0)),
                  pl.BlockSpec((None, 8, w), lambda b, i: (b, jnp.minimum(i * r8 + r8, ntot // 8 - 1), 0)),
                  pl.BlockSpec(conv_w8.shape, lambda b, i: (0, 0)),
                  pl.BlockSpec(ones_a.shape, lambda b, i: (0, 0))],
        out_specs=[row(A_QK_W), row(A_QK_W), row(A_W)],
        out_shape=[jax.ShapeDtypeStruct((nb, ntot, A_QK_W), F32)] * 3,
        scratch_shapes=[pltpu.VMEM((TM + 16, w), F32)],
        compiler_params=_params(("parallel", "arbitrary")),
        name="dn_prep",
    )(zqkv, zqkv, zqkv, conv_w8, ones_a)


def _dn_pre(d, rows, q_ref, k_ref, v_ref, ab_ref, alog, dtb, eg_ref, eb_ref):
    c = DN_CHUNK
    q = q_ref[rows, :]
    k = k_ref[rows, :]
    v = v_ref[rows, :]
    ab = ab_ref[rows, :]
    g = -jnp.exp(alog) * _softplus(ab + dtb)
    beta = jax.nn.sigmoid(ab)
    ri = lax.broadcasted_iota(jnp.int32, (c, 2 * c), 0)
    ci = lax.broadcasted_iota(jnp.int32, (c, 2 * c), 1) % c
    if d == 0:
        incl, strict = ri >= ci, ri > ci
    else:
        incl, strict = ri <= ci, ri < ci
    cum = jnp.where(incl[:, :c], 1.0, 0.0).astype(BF16)
    gc = _dot_exact_lhs(cum, g)
    last = c - 1 if d == 0 else 0
    g_last = gc[last:last + 1, :]
    egc = jnp.exp(gc)
    ekd = jnp.exp(g_last - gc)
    egl = jnp.broadcast_to(jnp.exp(g_last), (8, LANE))
    eg = eg_ref[d]
    eb = eb_ref[d]
    beta_x = _dot_exact_rhs(beta, eb)
    gx = _dot_exact_rhs(jnp.concatenate([egc, ekd, gc, egl], axis=0), eg)
    egc_x, ekd_x, gc_x, egl_x = gx[0:c], gx[c:2 * c], gx[2 * c:3 * c], gx[3 * c:3 * c + 1]
    gc_t = jnp.concatenate([gc, gc], axis=0).T
    kbeta = k * beta_x
    vbeta = v * beta_x
    wrhs = kbeta * egc_x
    qd = q * egc_x
    kd = k * ekd_x
    lane = lax.broadcasted_iota(jnp.int32, (1, LANE), 1)
    lo = lane < DN_DK
    pairs = []
    for p in range(DN_HEADS // 2):
        sl = slice(LANE * p, LANE * p + LANE)
        k_s, q_s = k[:, sl], q[:, sl]
        kb_s, vb_s, wr_s = kbeta[:, sl], vbeta[:, sl], wrhs[:, sl]
        gcx_s = gc_x[:, sl]
        k_sb = k_s.astype(BF16)
        k_sb2 = jnp.concatenate([k_sb, k_sb], axis=0)
        mats, rhss, attns = [], [], []
        for j in range(2):
            h = 2 * p + j
            mine = lo if j == 0 else jnp.logical_not(lo)
            gcx_r = pltpu.roll(gcx_s, DN_DK, 1)
            gcol = jnp.where(lo, gcx_s, gcx_r) if j == 0 else jnp.where(lo, gcx_r, gcx_s)
            grow = gc_t[DN_HEADS * d + h:DN_HEADS * d + h + 1, :]
            diff = gcol - grow
            dec = jnp.where(incl, jnp.exp(jnp.where(incl, diff, 0.0)), 0.0)
            kk = _dot_nt(jnp.where(mine, kb_s, 0.0).astype(BF16), k_sb2)
            mats.append(jnp.where(strict, kk * dec, 0.0))
            qk = _dot_nt(jnp.where(mine, q_s, 0.0).astype(BF16), k_sb)
            attns.append((qk * dec[:, :c]).astype(BF16))
            if j == 0:
                rhss.append(jnp.where(lo, vb_s, pltpu.roll(wr_s, DN_DK, 1)))
            else:
                rhss.append(jnp.where(lo, pltpu.roll(vb_s, DN_DK, 1), wr_s))
        pairs.append(dict(mats=mats, rhss=rhss, attns=attns, qd=qd[:, sl].astype(BF16),
                          kd_t=kd[:, sl].T.astype(BF16), egl=egl_x[:, sl]))
    return pairs


def _dn_solve(mats, rhss):
    c = DN_CHUNK
    lo = lax.broadcasted_iota(jnp.int32, (1, 2 * c), 1) < c
    pw, xs = list(mats), list(rhss)
    rounds = 6
    for r in range(rounds):
        for n in range(len(pw)):
            ph = pw[n].astype(BF16)
            plo = (pw[n] - ph.astype(F32)).astype(BF16)
            lhs = jnp.concatenate([jnp.where(lo, ph, plo), ph[:, :c]], axis=1)
            b = jnp.concatenate([xs[n], pw[n]], axis=1) if r < rounds - 1 else xs[n]
            bh = b.astype(BF16)
            bl = (b - bh.astype(F32)).astype(BF16)
            both = _dot(lhs, jnp.concatenate([bh, bh, bl], axis=0))
            px = both[:, :2 * c]
            if r < rounds - 1:
                pw[n] = both[:, 2 * c:]
            xs[n] = xs[n] - px if r == 0 else xs[n] + px
    return xs


def _dn_scan_kernel(alog_ref, dtb_ref, eg_ref, eb_ref,
                    qf, kf, vf, abf, qb, kb, vb, abb, of_ref, ob_ref, s_ref):
    i = pl.program_id(1)

    @pl.when(i == 0)
    def _():
        s_ref[...] = jnp.zeros_like(s_ref)

    alog = alog_ref[...]
    dtb = dtb_ref[...]
    nch = TM // DN_CHUNK
    npair = DN_HEADS // 2
    lane = lax.broadcasted_iota(jnp.int32, (1, LANE), 1)
    lo = lane < DN_DK
    ri2 = lax.broadcasted_iota(jnp.int32, (LANE, LANE), 0)
    ci2 = lax.broadcasted_iota(jnp.int32, (LANE, LANE), 1)
    bdiag = (ri2 < DN_DK) == (ci2 < DN_DK)

    def body(cidx, carry):
        rf = pl.ds(pl.multiple_of(cidx * DN_CHUNK, DN_CHUNK), DN_CHUNK)
        rb = pl.ds(pl.multiple_of((nch - 1 - cidx) * DN_CHUNK, DN_CHUNK), DN_CHUNK)
        pairs = (_dn_pre(0, rf, qf, kf, vf, abf, alog, dtb, eg_ref, eb_ref)
                 + _dn_pre(1, rb, qb, kb, vb, abb, alog, dtb, eg_ref, eb_ref))
        xs = _dn_solve([m for pr in pairs for m in pr["mats"]], [r for pr in pairs for r in pr["rhss"]])
        for n, pr in enumerate(pairs):
            d, p = divmod(n, npair)
            x0, x1 = xs[2 * n], xs[2 * n + 1]
            u = jnp.where(lo, x0, pltpu.roll(x1, DN_DK, 1))
            w = jnp.where(lo, pltpu.roll(x0, DN_DK, 1), x1)
            s = s_ref[n]
            sb = s.astype(BF16)
            v_new = u - _dot(w.astype(BF16), sb)
            vn_b = v_new.astype(BF16)
            o = _dot(pr["qd"], sb)
            o = o + _dot(pr["attns"][0], jnp.where(lo, vn_b, jnp.zeros_like(vn_b)))
            o = o + _dot(pr["attns"][1], jnp.where(lo, jnp.zeros_like(vn_b), vn_b))
            upd = _dot(pr["kd_t"], vn_b)
            s_ref[n] = s * pr["egl"] + jnp.where(bdiag, upd, 0.0)
            o_ref, rows = (of_ref, rf) if d == 0 else (ob_ref, rb)
            o_ref[rows, LANE * p:LANE * p + LANE] = o
        return carry

    lax.fori_loop(0, nch, body, 0)


def _dn_scan(q, k, v, ab, alog, dtb, eg, eb):
    nb, ntot, w = q.shape
    nt = ntot // TM
    nl = nt - 1
    fwd = lambda b, i: (b, jnp.where(i == 0, nl, i - 1), 0)
    bwd = lambda b, i: (b, jnp.where(i == 0, nl, nl - i), 0)
    const = lambda a: pl.BlockSpec(a.shape, lambda b, i: (0,) * a.ndim)
    blk = lambda ww, im: pl.BlockSpec((None, TM, ww), im)
    return pl.pallas_call(
        _dn_scan_kernel,
        grid=(nb, nt),
        in_specs=[const(alog), const(dtb), const(eg), const(eb),
                  blk(w, fwd), blk(w, fwd), blk(w, fwd), blk(LANE, fwd),
                  blk(w, bwd), blk(w, bwd), blk(w, bwd), blk(LANE, bwd)],
        out_specs=[blk(w, fwd), blk(w, bwd)],
        out_shape=[jax.ShapeDtypeStruct((nb, ntot, w), F32)] * 2,
        scratch_shapes=[pltpu.VMEM((2 * (DN_HEADS // 2), LANE, LANE), F32)],
        compiler_params=_params(("parallel", "arbitrary")),
        name="dn_scan",
    )(alog, dtb, eg, eb, q, k, v, ab, q, k, v, ab)


def _diff_attn_kernel(lam_init, q_ref, k_ref, vt_ref, lam_ref, g_ref, ones_ref, o_ref, st_ref):
    i = pl.program_id(1)
    nl = pl.num_programs(1) - 1
    lv = lam_ref[...]
    lam = (jnp.exp(jnp.sum(lv[0:1] * lv[1:2], axis=-1, keepdims=True))
           - jnp.exp(jnp.sum(lv[2:3] * lv[3:4], axis=-1, keepdims=True)) + lam_init)
    lane = lax.broadcasted_iota(jnp.int32, (1, B_QK_W), 1)

    nsm = 2 * DA_HEADS

    def run(tiles):
        q = q_ref[...]
        ones16 = jnp.ones((16, TM), BF16)
        m_prev = None
        res = []
        for n in range(nsm + 1):
            if n < nsm:
                slot = slice(DA_QK * n // LANE * LANE, DA_QK * n // LANE * LANE + LANE)
                lo = DA_QK * n
                qm = jnp.where(jnp.logical_and(lane >= lo, lane < lo + DA_QK), q, jnp.zeros_like(q))[:, slot]
            hp = (n - 1) // 2
            m8 = jnp.full((8, TM), -jnp.inf, F32)
            acc = jnp.zeros((DA_V + 16, TM), F32)
            for c in tiles:
                rows = slice(c * TM, c * TM + TM)
                if n < nsm:
                    st = _dot_nt(k_ref[rows, slot], qm)
                    st_ref[n % 2, rows, :] = st
                    m8 = jnp.maximum(m8, jnp.max(st.reshape(TM // 8, 8, TM), axis=0))
                if n > 0:
                    e = jnp.exp2(st_ref[(n - 1) % 2, rows, :] - m_prev).astype(BF16)
                    lhs = jnp.concatenate([vt_ref[c, DA_V * hp:DA_V * hp + DA_V, :], ones16], axis=0)
                    acc = acc + _dot(lhs, e)
            if n > 0:
                res.append(acc[:DA_V] / acc[DA_V:DA_V + 1])
            if n < nsm:
                m_prev = jnp.max(m8, axis=0, keepdims=True)
        ot = jnp.concatenate([res[2 * h] - lam * res[2 * h + 1] for h in range(DA_HEADS)], axis=0)
        ms = _dot_exact_lhs(ones_ref[...], ot * ot, parts=2)
        yt = (ot * lax.rsqrt(ms + EPS)) * g_ref[...]
        o_ref[...] = (yt * (1.0 - lam_init)).T.astype(BF16)

    ntiles = st_ref.shape[1] // TM

    @pl.when(i < nl)
    def _():
        run(range(ntiles))

    @pl.when(i == nl)
    def _():
        run([ntiles - 1])


def _diff_attn(q, k, vt, lam_vecs, subln_g, ones_b, lam_init):
    nb, ntot, w = q.shape
    nt = ntot // TM
    row = pl.BlockSpec((None, TM, w), lambda b, i: (b, i, 0))
    full = pl.BlockSpec((None, ntot, w), lambda b, i: (b, 0, 0))
    full_t = pl.BlockSpec((None,) + vt.shape[1:], lambda b, i: (b, 0, 0, 0))
    const = lambda a: pl.BlockSpec(a.shape, lambda b, i: (0,) * a.ndim)
    return pl.pallas_call(
        functools.partial(_diff_attn_kernel, lam_init),
        grid=(nb, nt),
        in_specs=[row, full, full_t, const(lam_vecs), const(subln_g), const(ones_b)],
        out_specs=row,
        out_shape=jax.ShapeDtypeStruct((nb, ntot, w), BF16),
        scratch_shapes=[pltpu.VMEM((2, ntot, TM), F32)],
        compiler_params=_params(("parallel", "arbitrary")),
        name="diff_attn",
    )(q, k, vt, lam_vecs, subln_g, ones_b)


def _win_attn_kernel(n_lat, sink_ref, q_ref, k_ref, v_ref, o_ref):
    i = pl.program_id(1)
    nl = pl.num_programs(1) - 1
    rep = WA_HEADS // WA_KV_HEADS
    lane = lax.broadcasted_iota(jnp.int32, (1, LANE), 1)
    lo = lane < WA_DIM
    rowg = lax.broadcasted_iota(jnp.int32, (rep * TM, 1), 0) // TM

    def run(k_all, v_all, bias):
        outs = []
        for g in range(WA_KV_HEADS):
            mine = lo if g == 0 else jnp.logical_not(lo)
            q3 = jnp.concatenate(
                [jnp.where(mine, q_ref[:, LANE * s:LANE * s + LANE], jnp.zeros((TM, LANE), BF16))
                 for s in range(rep)], axis=0)
            s = _dot_nt(q3, k_all)
            if bias is not None:
                s = s + bias
            sk = jnp.zeros((rep * TM, 1), F32)
            for r in range(rep):
                sk = jnp.where(rowg == r, sink_ref[rep * g + r], sk)
            m = jnp.maximum(jnp.max(s, axis=-1, keepdims=True), sk)
            e = jnp.exp(s - m)
            den = jnp.sum(e, axis=-1, keepdims=True) + jnp.exp(sk - m)
            outs.append(_dot(e.astype(BF16), v_all) / den)
        for s in range(rep):
            o_ref[:, LANE * s:LANE * s + LANE] = jnp.where(
                lo, outs[0][TM * s:TM * s + TM], outs[1][TM * s:TM * s + TM]).astype(BF16)

    kc = k_ref[n_lat:, :]
    vc = v_ref[n_lat:, :]
    band = 3 * TM

    @pl.when(i < nl)
    def _():
        start = pl.multiple_of(jnp.clip((i - 1) * TM, 0, n_lat - band), TM)
        kb = k_ref[pl.ds(start, band), :]
        vb = v_ref[pl.ds(start, band), :]
        qpos = i * TM + lax.broadcasted_iota(jnp.int32, (TM, 1), 0)
        kpos = start + lax.broadcasted_iota(jnp.int32, (1, band), 1)
        near = jnp.where(jnp.abs(qpos - kpos) <= WINDOW, 0.0, -1e30)
        bias = jnp.concatenate([near, jnp.zeros((TM, kc.shape[0]), F32)], axis=1)
        bias = jnp.concatenate([bias] * rep, axis=0)
        run(jnp.concatenate([kb, kc], axis=0), jnp.concatenate([vb, vc], axis=0), bias)

    @pl.when(i == nl)
    def _():
        run(kc, vc, None)


def _win_attn(q, k, v, sink, n_lat):
    nb, ntot, w = q.shape
    nt = ntot // TM
    kw = k.shape[-1]
    row = pl.BlockSpec((None, TM, w), lambda b, i: (b, i, 0))
    full = pl.BlockSpec((None, ntot, kw), lambda b, i: (b, 0, 0))
    return pl.pallas_call(
        functools.partial(_win_attn_kernel, n_lat),
        grid=(nb, nt),
        in_specs=[pl.BlockSpec(memory_space=pltpu.SMEM), row, full, full],
        out_specs=row,
        out_shape=jax.ShapeDtypeStruct((nb, ntot, w), BF16),
        compiler_params=_params(("parallel", "arbitrary")),
        name="win_attn",
    )(sink, q, k, v)


def _out_proj_kernel(x_ref, of_ref, ob_ref, gate_ref, yb_ref, yw_ref, wa_ref, wb_ref, wc_ref, dng_ref,
                     ones_ref, g1_ref, sc2_ref, sh2_ref, n2_ref, wr_ref, br_ref, xn_o, h2_o, lg_o):
    o = of_ref[...] + ob_ref[...]
    ms = _dot_exact_rhs(o * o, ones_ref[...], parts=2)
    ya = ((o * lax.rsqrt(ms + EPS)) * dng_ref[...]) * _silu(gate_ref[...])
    y = _dot(ya.astype(BF16), wa_ref[...]) + _dot(yb_ref[...], wb_ref[...]) + _dot(yw_ref[...], wc_ref[...])
    xn = x_ref[...] + g1_ref[...] * y
    xn_o[...] = xn
    h2 = _rms_mod(xn, n2_ref[...], sc2_ref[...], sh2_ref[...])
    h2_o[...] = h2.astype(BF16)
    lg_o[...] = _dot3(h2, wr_ref[...]) + br_ref[...]


def _out_proj(x, of, ob, gate, yb, yw, wa, wb, wc, dng, ones_a, mod, n2g, wr, br, nb, ntot):
    d = x.shape[-1]
    nt = ntot // TM
    nl = nt - 1
    row = lambda w: pl.BlockSpec((None, TM, w), lambda b, i: (b, i, 0))
    const = lambda a: pl.BlockSpec(a.shape, lambda b, i: (0,) * a.ndim)
    return pl.pallas_call(
        _out_proj_kernel,
        grid=(nb, nt),
        in_specs=[row(d), row(A_W), row(A_W), row(A_W), row(B_W), row(C_W),
                  const(wa), const(wb), const(wc), const(dng), const(ones_a),
                  _mod_spec_d(2, nb, nl, d), _mod_spec_d(4, nb, nl, d), _mod_spec_d(3, nb, nl, d),
                  const(n2g), const(wr), const(br)],
        out_specs=[row(d), row(d), row(LANE)],
        out_shape=[jax.ShapeDtypeStruct((nb, ntot, d), F32), jax.ShapeDtypeStruct((nb, ntot, d), BF16),
                   jax.ShapeDtypeStruct((nb, ntot, LANE), F32)],
        compiler_params=_params(("parallel", "arbitrary")),
        name="out_proj",
    )(x, of, ob, gate, yb, yw, wa, wb, wc, dng, ones_a, mod, mod, mod, n2g, wr, br)


def _route_kernel(lg_ref, gw_o, pos_o, cnt_o):
    t = lg_ref.shape[0]
    lg = lg_ref[...]
    lane_i = lax.broadcasted_iota(jnp.int32, (1, LANE), 1)
    lane = lane_i.astype(F32)
    big = float(LANE)
    neg = -jnp.inf
    is_g = lane_i < N_GROUPS
    lgm = jnp.where(is_g, lg, neg)
    mg = jnp.max(lgm, axis=-1, keepdims=True)
    p_sel = 1.0 / jnp.sum(jnp.where(is_g, jnp.exp(lgm - mg), 0.0), axis=-1, keepdims=True)
    gidx = jnp.min(jnp.where(jnp.logical_and(is_g, lgm == mg), lane, big), axis=-1, keepdims=True)
    e_lane = lane_i - N_GROUPS
    in_grp = jnp.logical_and(jnp.logical_and(e_lane >= 0, e_lane < N_EXPERTS),
                             jnp.floor((lane - N_GROUPS) * (1.0 / EXP_PER_GROUP)) == gidx)
    le = jnp.where(in_grp, lg, neg)
    v1 = jnp.max(le, axis=-1, keepdims=True)
    i1 = jnp.min(jnp.where(jnp.logical_and(in_grp, le == v1), lane, big), axis=-1, keepdims=True)
    is1 = lane == i1
    le2 = jnp.where(is1, neg, le)
    v2 = jnp.max(le2, axis=-1, keepdims=True)
    rest = jnp.logical_and(in_grp, jnp.logical_not(is1))
    i2 = jnp.min(jnp.where(jnp.logical_and(rest, le2 == v2), lane, big), axis=-1, keepdims=True)
    is2 = lane == i2
    e2 = jnp.exp(v2 - v1)
    w1 = 1.0 / (1.0 + e2)
    w2 = e2 / (1.0 + e2)
    gw = jnp.where(is1, p_sel * w1, jnp.where(is2, p_sel * w2, 0.0))
    sel = jnp.logical_or(is1, is2)
    self_ = jnp.where(sel, 1.0, 0.0)
    ri = lax.broadcasted_iota(jnp.int32, (ROUTE_BLK, ROUTE_BLK), 0)
    ci = lax.broadcasted_iota(jnp.int32, (ROUTE_BLK, ROUTE_BLK), 1)
    tri = jnp.where(ri > ci, 1.0, 0.0).astype(BF16)
    run = jnp.zeros((1, LANE), F32)
    pos_blocks = []
    for b in range(t // ROUTE_BLK):
        blk = self_[b * ROUTE_BLK:(b + 1) * ROUTE_BLK]
        pos_blocks.append(_dot(tri, blk.astype(BF16)) + run)
        run = run + jnp.sum(blk, axis=0, keepdims=True)
    pos = jnp.where(sel, jnp.concatenate(pos_blocks, axis=0), -1.0)
    gw_o[...] = gw.T
    pos_o[...] = pos.T
    cnt_o[...] = jnp.broadcast_to(run, (8, LANE)).astype(jnp.int32)


def _route(logits):
    ntok = logits.shape[0]
    ntile = ntok // MOE_T
    return pl.pallas_call(
        _route_kernel,
        grid=(ntile,),
        in_specs=[pl.BlockSpec((MOE_T, LANE), lambda t: (t, 0))],
        out_specs=[pl.BlockSpec((None, LANE, MOE_T), lambda t: (t, 0, 0)),
                   pl.BlockSpec((None, LANE, MOE_T), lambda t: (t, 0, 0)),
                   pl.BlockSpec((None, 8, LANE), lambda t: (t, 0, 0))],
        out_shape=[jax.ShapeDtypeStruct((ntile, LANE, MOE_T), F32),
                   jax.ShapeDtypeStruct((ntile, LANE, MOE_T), F32),
                   jax.ShapeDtypeStruct((ntile, 8, LANE), jnp.int32)],
        compiler_params=_params(("parallel",)),
        name="moe_route",
    )(logits)


def _moe_kernel(cnt_ref, h_ref, pos_ref, gw_ref, wg_ref, wu_ref, wd_ref, o_ref):
    t = pl.program_id(0)
    e = pl.program_id(1)

    @pl.when(e == 0)
    def _():
        o_ref[...] = jnp.zeros_like(o_ref)

    n = cnt_ref[t * N_EXPERTS + e]
    prow = pos_ref[pl.ds(N_GROUPS + e, 1), :]
    grow = gw_ref[pl.ds(N_GROUPS + e, 1), :]

    def chunk(c, carry):
        slot = (lax.broadcasted_iota(jnp.int32, (MOE_CH, 1), 0) + c * MOE_CH).astype(F32)
        hit = prow == slot
        onehot = jnp.where(hit, 1.0, 0.0).astype(BF16)
        hc = _dot(onehot, h_ref[...]).astype(BF16)
        a = _dot(hc, wg_ref[...])
        b = _dot(hc, wu_ref[...])
        y = _dot((_silu(a) * b).astype(BF16), wd_ref[...])
        gcol = jnp.sum(jnp.where(hit, grow, 0.0), axis=-1, keepdims=True)
        o_ref[...] += _dot_tn(onehot, (y * gcol).astype(BF16))
        return carry

    lax.fori_loop(0, (n + MOE_CH - 1) // MOE_CH, chunk, 0)


def _moe(counts, h2, pos_t, gw_t, wg, wu, wd):
    ntok, d = h2.shape
    ntile = ntok // MOE_T
    gs = pltpu.PrefetchScalarGridSpec(
        num_scalar_prefetch=1,
        grid=(ntile, N_EXPERTS),
        in_specs=[pl.BlockSpec((MOE_T, d), lambda t, e, c: (t, 0)),
                  pl.BlockSpec((None, LANE, MOE_T), lambda t, e, c: (t, 0, 0)),
                  pl.BlockSpec((None, LANE, MOE_T), lambda t, e, c: (t, 0, 0)),
                  pl.BlockSpec((None, d, D_EXPERT), lambda t, e, c: (e, 0, 0)),
                  pl.BlockSpec((None, d, D_EXPERT), lambda t, e, c: (e, 0, 0)),
                  pl.BlockSpec((None, D_EXPERT, d), lambda t, e, c: (e, 0, 0))],
        out_specs=pl.BlockSpec((MOE_T, d), lambda t, e, c: (t, 0)),
    )
    return pl.pallas_call(
        _moe_kernel,
        grid_spec=gs,
        out_shape=jax.ShapeDtypeStruct((ntok, d), F32),
        compiler_params=_params(("parallel", "arbitrary")),
        name="moe_experts",
    )(counts, h2, pos_t, gw_t, wg, wu, wd)


def _final_kernel(xn_ref, ff_ref, g2_ref, g_ref, o_ref):
    x = xn_ref[...] + g2_ref[...] * ff_ref[...]
    y = x * lax.rsqrt(jnp.mean(x * x, axis=-1, keepdims=True) + EPS)
    o_ref[...] = y * g_ref[...]


def _final(xn, ff, mod, g, nb, n_lat, ntot):
    d = xn.shape[-1]
    nl = n_lat // TM
    row = pl.BlockSpec((None, TM, d), lambda b, i: (b, i, 0))
    return pl.pallas_call(
        _final_kernel,
        grid=(nb, nl),
        in_specs=[row, row, _mod_spec_d(5, nb, nl, d), pl.BlockSpec(g.shape, lambda b, i: (0, 0))],
        out_specs=row,
        out_shape=jax.ShapeDtypeStruct((nb, n_lat, d), F32),
        compiler_params=_params(("parallel", "arbitrary")),
        name="final_norm",
    )(xn, ff, mod, g)


def _rope_swap_perm(width, dim):
    nf = dim // 4
    j = jnp.arange(width)
    base = (j // (2 * nf)) * (2 * nf)
    return base + (j % (2 * nf) + nf) % (2 * nf)


def _rope_tables(n_lat, n_ctx, dim, width):
    nf = dim // 4
    t = jnp.arange(n_lat)
    row = (t // GRID_W).astype(F32)
    col = (t % GRID_W).astype(F32)
    inv = ROPE_BASE ** (-jnp.arange(nf, dtype=F32) / nf)
    j = jnp.arange(width) % dim
    axis = j // (2 * nf)
    pos = jnp.where(axis[None, :] == 0, row[:, None], col[:, None])
    ang = pos * inv[j % nf][None, :]
    sign = jnp.where(j % (2 * nf) < nf, -1.0, 1.0).astype(F32)
    cos = jnp.concatenate([jnp.cos(ang), jnp.ones((n_ctx, width), F32)], axis=0)
    sin = jnp.concatenate([jnp.sin(ang) * sign[None, :], jnp.zeros((n_ctx, width), F32)], axis=0)
    return cos, sin


def _block_ones(width, group, value):
    j = jnp.arange(width)
    return jnp.where((j[:, None] // group) == (j[None, :] // group), value, 0.0).astype(BF16)


def _slot_cols():
    rep = WA_HEADS // WA_KV_HEADS
    cols = []
    for s in range(rep):
        cols.append(jnp.arange(WA_DIM) + WA_DIM * s)
        cols.append(jnp.arange(WA_DIM) + WA_DIM * (rep + s))
    return jnp.concatenate(cols)


def _build_w_in(w):
    d = w.shape[0]
    s1, s2 = A_COLS, A_COLS + B_COLS
    wa, wb, wc = w[:, :s1], w[:, s1:s2], w[:, s2:]
    ab = jnp.pad(wa[:, QKV_W + A_W:], ((0, 0), (0, LANE - 4 * DN_HEADS)))
    bq, bk, bv = wb[:, :B_QK_W], wb[:, B_QK_W:2 * B_QK_W], wb[:, 2 * B_QK_W:]
    pb = _rope_swap_perm(B_QK_W, DA_QK)
    cq = wc[:, :C_W][:, _slot_cols()]
    ck, cv = wc[:, C_W:C_W + C_KV_W], wc[:, C_W + C_KV_W:]
    pcq = _rope_swap_perm(C_W, WA_DIM)
    pck = _rope_swap_perm(C_KV_W, WA_DIM)
    cat = jnp.concatenate([wa[:, :QKV_W], wa[:, QKV_W:QKV_W + A_W], ab,
                           bq, bk, bv, bq[:, pb], bk[:, pb],
                           cq, ck, cv, cq[:, pcq], ck[:, pck]], axis=1)
    assert cat.shape == (d, _O_END)
    return cat.astype(BF16)


def _expand_mats():
    r = jnp.arange(LANE)[:, None]
    h = (jnp.arange(A_QK_W) // DN_DK)[None, :]
    eg = jnp.stack([(r == DN_HEADS * d + h) for d in range(2)]).astype(BF16)
    eb = jnp.stack([(r == 2 * DN_HEADS + DN_HEADS * d + h) for d in range(2)]).astype(BF16)
    return eg, eb


def kernel(x, c, ctx, c_ctx, ada_w, ada_b, norm1_g, norm2_g, w_in, dn_conv_w, dn_a_log, dn_dt_bias, dn_norm_g, da_lambda, da_subln_g, wa_sink, w_out, router_group_w, router_group_b, router_expert_w, router_expert_b, exp_w_gate, exp_w_up, exp_w_down, final_norm_g):
    nb, n_lat, d = x.shape
    n_ctx = ctx.shape[1]
    depth = ada_w.shape[0]
    assert n_ctx == TM and n_lat % TM == 0 and n_lat >= 3 * TM
    ntot = n_lat + n_ctx
    ntok = nb * ntot
    assert ntok % MOE_T == 0

    xa = jnp.concatenate([x, ctx], axis=1)
    cc = jnp.zeros((16, d), F32).at[:nb].set(c).at[nb].set(c_ctx)
    cosb, sinb = _rope_tables(n_lat, n_ctx, DA_QK, B_QK_W)
    cosc, sinc = _rope_tables(n_lat, n_ctx, WA_DIM, C_KV_W)
    tabs = (cosb, sinb, cosc, sinc)
    ones_a = _block_ones(A_W, DN_DV, 1.0)
    mean_a = _block_ones(A_W, DN_DV, 1.0 / DN_DV)
    mean_b = _block_ones(B_W, DA_V, 1.0 / DA_V)
    eg, eb = _expand_mats()
    slot_rows = _slot_cols()
    pad_lane = lambda v: jnp.pad(v.reshape(1, -1), ((0, 0), (0, LANE - v.size)))

    mods = [_ada(cc, ada_w[li], ada_b[li]).reshape(16, 6, 1, d) for li in range(depth)]
    xn = ff = None
    for li in range(depth):
        mod = mods[li]
        w_cat = _build_w_in(w_in[li])
        n1g = norm1_g[li].reshape(1, d)
        if li == 0:
            outs = _in_proj((xa,), (mod,), n1g, w_cat, tabs, nb, ntot)
            xcur = xa
        else:
            outs = _in_proj((xn, ff.reshape(nb, ntot, d)), (mod, mods[li - 1]), n1g, w_cat, tabs, nb, ntot)
            xcur, outs = outs[0], outs[1:]
        zqkv, gate, ab, qb, kb, vb, qc, kc, vc = outs

        conv_w8 = jnp.pad(dn_conv_w[li], ((0, 8 - DN_CONV), (0, 0)))
        q, k, v = _dn_prep(zqkv, conv_w8, ones_a)
        of, ob = _dn_scan(q, k, v, ab, pad_lane(dn_a_log[li]), pad_lane(dn_dt_bias[li]), eg, eb)

        lam_init = 0.8 - 0.6 * math.exp(-0.3 * li)
        yb = _diff_attn(qb, kb, vb, da_lambda[li], jnp.tile(da_subln_g[li], DA_HEADS).reshape(B_W, 1),
                        mean_b, lam_init)
        yw = _win_attn(qc, kc, vc, jnp.pad(wa_sink[li], (0, 8 - WA_HEADS)), n_lat)

        wo = w_out[li]
        wa_o = wo[:A_W].astype(BF16)
        wb_o = wo[A_W:A_W + B_W].astype(BF16)
        wc_o = wo[A_W + B_W:][slot_rows].astype(BF16)
        wr = jnp.pad(jnp.concatenate([router_group_w[li], router_expert_w[li]], axis=1),
                     ((0, 0), (0, LANE - N_GROUPS - N_EXPERTS)))
        br = pad_lane(jnp.concatenate([router_group_b[li], router_expert_b[li]]))
        xn, h2, logits = _out_proj(xcur, of, ob, gate, yb, yw, wa_o, wb_o, wc_o,
                                   jnp.tile(dn_norm_g[li], DN_HEADS).reshape(1, A_W), mean_a, mod,
                                   norm2_g[li].reshape(1, d), wr, br, nb, ntot)

        gw_t, pos_t, cnt = _route(logits.reshape(ntok, LANE))
        counts = cnt[:, 0, N_GROUPS:N_GROUPS + N_EXPERTS].reshape(-1)
        ff = _moe(counts, h2.reshape(ntok, d), pos_t, gw_t, exp_w_gate[li].astype(BF16),
                  exp_w_up[li].astype(BF16), exp_w_down[li].astype(BF16))

    return _final(xn, ff.reshape(nb, ntot, d), mods[depth - 1], final_norm_g.reshape(1, d), nb, n_lat, ntot)
```

```python
import functools
import math

import jax
import jax.numpy as jnp
from jax import lax
from jax.experimental import pallas as pl
from jax.experimental.pallas import tpu as pltpu

F32 = jnp.float32
BF16 = jnp.bfloat16

GRID_W = 64
EPS = 1e-6
LOG2E = math.log2(math.e)
ROPE_BASE = 10000.0
DN_HEADS = 6
DN_DK = 64
DN_DV = 64
DN_CONV = 5
DN_CHUNK = 64
DA_HEADS = 4
DA_QK = 32
DA_V = 64
WA_HEADS = 6
WA_KV_HEADS = 2
WA_DIM = 64
WINDOW = 128
N_GROUPS = 4
EXP_PER_GROUP = 4
N_EXPERTS = 16
D_EXPERT = 512

A_QK_W = DN_HEADS * DN_DK
A_W = DN_HEADS * DN_DV
QKV_W = 2 * A_QK_W + A_W
B_W = DA_HEADS * DA_V
B_QK_W = 2 * DA_HEADS * DA_QK
C_W = WA_HEADS * WA_DIM
C_KV_W = WA_KV_HEADS * WA_DIM
A_COLS = QKV_W + A_W + 4 * DN_HEADS
B_COLS = 2 * B_QK_W + B_W

LANE = 128
TM = 256
MOE_T = 1024
MOE_CH = 192
ROUTE_BLK = 256
DN_EXACT_ROUNDS = 5
DA_KEY_GROUP = 4
DN_GROUP = 4
VMEM_LIMIT = 56 * 1024 * 1024

_O_QKV = 0
_O_GATE = _O_QKV + QKV_W
_O_AB = _O_GATE + A_W
_O_BQ = _O_AB + LANE
_O_BK = _O_BQ + B_QK_W
_O_BV = _O_BK + B_QK_W
_O_BQS = _O_BV + B_W
_O_BKS = _O_BQS + B_QK_W
_O_CQ = _O_BKS + B_QK_W
_O_CK = _O_CQ + C_W
_O_CV = _O_CK + C_KV_W
_O_CQS = _O_CV + C_KV_W
_O_CKS = _O_CQS + C_W
_O_END = _O_CKS + C_KV_W


def _dot(a, b):
    return jnp.dot(a, b, preferred_element_type=F32)


def _dot_nt(a, b):
    return lax.dot_general(a, b, (((1,), (1,)), ((), ())), preferred_element_type=F32)


def _dot_tn(a, b):
    return lax.dot_general(a, b, (((0,), (0,)), ((), ())), preferred_element_type=F32)


def _split2(a):
    hi = a.astype(BF16)
    lo = (a - hi.astype(F32)).astype(BF16)
    return hi, lo


def _split3(a):
    hi = a.astype(BF16)
    r = a - hi.astype(F32)
    mid = r.astype(BF16)
    lo = (r - mid.astype(F32)).astype(BF16)
    return hi, mid, lo


def _dot3(a, b):
    ah, al = _split2(a)
    bh, bl = _split2(b)
    return _dot(ah, bh) + (_dot(ah, bl) + _dot(al, bh))


def _dot_exact_rhs(a, b_bf16, parts=3):
    sp = _split3(a) if parts == 3 else _split2(a)
    out = _dot(sp[0], b_bf16)
    for p in sp[1:]:
        out = out + _dot(p, b_bf16)
    return out


def _dot_exact_lhs(a_bf16, b, parts=3):
    sp = _split3(b) if parts == 3 else _split2(b)
    out = _dot(a_bf16, sp[0])
    for p in sp[1:]:
        out = out + _dot(a_bf16, p)
    return out


def _col_reduce(x, op, slab=64):
    n, w = x.shape
    if n > slab and n % slab == 0:
        x = op(x.reshape(n // slab, slab, w), axis=0)
    return op(x, axis=0, keepdims=True)


def _silu(x):
    return x * jax.nn.sigmoid(x)


def _softplus(x):
    return jnp.maximum(x, 0.0) + jnp.log1p(jnp.exp(-jnp.abs(x)))


def _params(sem):
    return pltpu.CompilerParams(dimension_semantics=sem, vmem_limit_bytes=VMEM_LIMIT)


def _ada_kernel(c_ref, w_ref, b_ref, o_ref):
    o_ref[...] = _dot3(_silu(c_ref[...]), w_ref[...]) + b_ref[...]


def _ada(cc, w, b):
    rows, d = cc.shape
    n = w.shape[1]
    tn = n // 4
    return pl.pallas_call(
        _ada_kernel,
        grid=(n // tn,),
        in_specs=[pl.BlockSpec((rows, d), lambda j: (0, 0)),
                  pl.BlockSpec((d, tn), lambda j: (0, j)),
                  pl.BlockSpec((1, tn), lambda j: (0, j))],
        out_specs=pl.BlockSpec((rows, tn), lambda j: (0, j)),
        out_shape=jax.ShapeDtypeStruct((rows, n), F32),
        compiler_params=_params(("arbitrary",)),
        name="ada_mod",
    )(cc, w, b.reshape(1, n))


def _mod_spec_d(k, nb, nl, d):
    return pl.BlockSpec((None, None, 1, d), lambda b, i: (jnp.where(i == nl, nb, b), k, 0, 0))


def _rms_mod(x, g, sc, sh):
    y = x * lax.rsqrt(jnp.mean(x * x, axis=-1, keepdims=True) + EPS)
    return (y * g) * (1.0 + sc) + sh


def _in_proj_kernel(fuse_res, *refs):
    if fuse_res:
        xn_ref, ff_ref, g2_ref = refs[:3]
        refs = refs[3:]
    else:
        x_ref = refs[0]
        refs = refs[1:]
    (sc_ref, sh_ref, g_ref, w_ref, cosb_ref, sinb_ref, cosc_ref, sinc_ref) = refs[:8]
    outs = refs[8:]
    if fuse_res:
        x = xn_ref[...] + g2_ref[...] * ff_ref[...]
        outs[0][...] = x
        outs = outs[1:]
    else:
        x = x_ref[...]
    (zqkv_o, gate_o, ab_o, qb_o, kb_o, vb_o, qc_o, kc_o, vc_o) = outs
    hb = _rms_mod(x, g_ref[...], sc_ref[...], sh_ref[...]).astype(BF16)

    def seg(a, b):
        return _dot(hb, w_ref[:, a:b])

    zqkv_o[...] = seg(_O_QKV, _O_GATE)
    gate_o[...] = seg(_O_GATE, _O_AB)
    ab_o[...] = seg(_O_AB, _O_BQ)
    cb = cosb_ref[...]
    sb = sinb_ref[...]
    qb_o[...] = ((seg(_O_BQ, _O_BK) * cb + seg(_O_BQS, _O_BKS) * sb) * (DA_QK ** -0.5 * LOG2E)).astype(BF16)
    kb_o[...] = (seg(_O_BK, _O_BV) * cb + seg(_O_BKS, _O_CQ) * sb).astype(BF16)
    vb_o[...] = seg(_O_BV, _O_BQS).T.astype(BF16)
    cc = cosc_ref[...]
    sc_ = sinc_ref[...]
    cc3 = jnp.concatenate([cc, cc, cc], axis=1)
    sc3 = jnp.concatenate([sc_, sc_, sc_], axis=1)
    qc_o[...] = ((seg(_O_CQ, _O_CK) * cc3 + seg(_O_CQS, _O_CKS) * sc3) * (WA_DIM ** -0.5)).astype(BF16)
    kc_o[...] = (seg(_O_CK, _O_CV) * cc + seg(_O_CKS, _O_END) * sc_).astype(BF16)
    vc_o[...] = seg(_O_CV, _O_CQS).astype(BF16)


def _in_proj(x_parts, mod, norm_g, w_cat, tabs, nb, ntot):
    fuse_res = len(x_parts) == 2
    d = x_parts[0].shape[-1]
    nt = ntot // TM
    nl = nt - 1
    row = lambda w: pl.BlockSpec((None, TM, w), lambda b, i: (b, i, 0))
    tab = lambda w: pl.BlockSpec((TM, w), lambda b, i: (i, 0))
    const = lambda a: pl.BlockSpec(a.shape, lambda b, i: (0,) * a.ndim)
    if fuse_res:
        (xn, ff), prev_mod = x_parts, mod[1]
        ins = [xn, ff, prev_mod]
        in_specs = [row(d), row(d), _mod_spec_d(5, nb, nl, d)]
        cur_mod = mod[0]
    else:
        ins = [x_parts[0]]
        in_specs = [row(d)]
        cur_mod = mod[0]
    ins += [cur_mod, cur_mod, norm_g, w_cat, *tabs]
    in_specs += [_mod_spec_d(1, nb, nl, d), _mod_spec_d(0, nb, nl, d), const(norm_g), const(w_cat),
                 tab(B_QK_W), tab(B_QK_W), tab(C_KV_W), tab(C_KV_W)]
    widths = [(QKV_W, F32), (A_W, F32), (LANE, F32), (B_QK_W, BF16), (B_QK_W, BF16), (None, BF16),
              (C_W, BF16), (C_KV_W, BF16), (C_KV_W, BF16)]
    if fuse_res:
        widths = [(d, F32)] + widths
    return pl.pallas_call(
        functools.partial(_in_proj_kernel, fuse_res),
        grid=(nb, nt),
        in_specs=in_specs,
        out_specs=[pl.BlockSpec((None, None, B_W, TM), lambda b, i: (b, i, 0, 0)) if w is None else row(w)
                   for w, _ in widths],
        out_shape=[jax.ShapeDtypeStruct((nb, nt, B_W, TM) if w is None else (nb, ntot, w), dt)
                   for w, dt in widths],
        compiler_params=_params(("parallel", "arbitrary")),
        name="in_proj",
    )(*ins)


def _dn_prep_kernel(zc_ref, zp_ref, zn_ref, w_ref, ones_ref, q_o, k_o, v_o, ext_ref):
    i = pl.program_id(1)
    nl = pl.num_programs(1) - 1
    prev_ok = jnp.logical_and(i >= 1, i <= nl - 1)
    next_ok = i <= nl - 2
    ext_ref[0:8, :] = jnp.where(prev_ok, zp_ref[...], 0.0)
    ext_ref[8:8 + TM, :] = zc_ref[...]
    ext_ref[8 + TM:16 + TM, :] = jnp.where(next_ok, zn_ref[...], 0.0)
    half = DN_CONV // 2
    acc = w_ref[0:1, :] * ext_ref[8 - half:8 - half + TM, :]
    for j in range(1, DN_CONV):
        acc = acc + w_ref[j:j + 1, :] * ext_ref[8 - half + j:8 - half + j + TM, :]
    y = _silu(acc)
    ones = ones_ref[...]

    def l2n(t):
        ss = _dot_exact_rhs(t * t, ones, parts=2)
        return t * lax.rsqrt(ss + EPS)

    q_o[...] = l2n(y[:, :A_QK_W]) * (DN_DK ** -0.5)
    k_o[...] = l2n(y[:, A_QK_W:2 * A_QK_W])
    v_o[...] = y[:, 2 * A_QK_W:]


def _dn_prep(zqkv, conv_w8, ones_a):
    nb, ntot, w = zqkv.shape
    nt = ntot // TM
    r8 = TM // 8
    row = lambda ww: pl.BlockSpec((None, TM, ww), lambda b, i: (b, i, 0))
    return pl.pallas_call(
        _dn_prep_kernel,
        grid=(nb, nt),
        in_specs=[row(w),
                  pl.BlockSpec((None, 8, w), lambda b, i: (b, jnp.maximum(i * r8 - 1, 0), 0)),
                  pl.BlockSpec((None, 8, w), lambda b, i: (b, jnp.minimum(i * r8 + r8, ntot // 8 - 1), 0)),
                  pl.BlockSpec(conv_w8.shape, lambda b, i: (0, 0)),
                  pl.BlockSpec(ones_a.shape, lambda b, i: (0, 0))],
        out_specs=[row(A_QK_W), row(A_QK_W), row(A_W)],
        out_shape=[jax.ShapeDtypeStruct((nb, ntot, A_QK_W), F32)] * 3,
        scratch_shapes=[pltpu.VMEM((TM + 16, w), F32)],
        compiler_params=_params(("parallel", "arbitrary")),
        name="dn_prep",
    )(zqkv, zqkv, zqkv, conv_w8, ones_a)


def _dn_pre(d, rows, q_ref, k_ref, v_ref, ab_ref, alog, dtb, eg_ref, eb_ref):
    c = DN_CHUNK
    q = q_ref[rows, :]
    k = k_ref[rows, :]
    v = v_ref[rows, :]
    ab = ab_ref[rows, :]
    g = -jnp.exp(alog) * _softplus(ab + dtb)
    beta = jax.nn.sigmoid(ab)
    ri = lax.broadcasted_iota(jnp.int32, (c, 2 * c), 0)
    ci = lax.broadcasted_iota(jnp.int32, (c, 2 * c), 1) % c
    if d == 0:
        incl, strict = ri >= ci, ri > ci
    else:
        incl, strict = ri <= ci, ri < ci
    cum = jnp.where(incl[:, :c], 1.0, 0.0).astype(BF16)
    gc = _dot_exact_lhs(cum, g)
    last = c - 1 if d == 0 else 0
    g_last = gc[last:last + 1, :]
    egc = jnp.exp(gc)
    ekd = jnp.exp(g_last - gc)
    egl = jnp.broadcast_to(jnp.exp(g_last), (8, LANE))
    eg = eg_ref[d]
    eb = eb_ref[d]
    beta_x = _dot_exact_rhs(beta, eb)
    gx = _dot_exact_rhs(jnp.concatenate([egc, ekd, gc, egl], axis=0), eg)
    egc_x, ekd_x, gc_x, egl_x = gx[0:c], gx[c:2 * c], gx[2 * c:3 * c], gx[3 * c:3 * c + 1]
    gc_t = jnp.concatenate([gc, gc], axis=0).T
    kbeta = k * beta_x
    vbeta = v * beta_x
    wrhs = kbeta * egc_x
    qd = q * egc_x
    kd = k * ekd_x
    lane = lax.broadcasted_iota(jnp.int32, (1, LANE), 1)
    lo = lane < DN_DK
    pairs = []
    for p in range(DN_HEADS // 2):
        sl = slice(LANE * p, LANE * p + LANE)
        k_s, q_s = k[:, sl], q[:, sl]
        kb_s, vb_s, wr_s = kbeta[:, sl], vbeta[:, sl], wrhs[:, sl]
        gcx_s = gc_x[:, sl]
        k_sb = k_s.astype(BF16)
        k_sb2 = jnp.concatenate([k_sb, k_sb], axis=0)
        mats, rhss, attns = [], [], []
        for j in range(2):
            h = 2 * p + j
            mine = lo if j == 0 else jnp.logical_not(lo)
            gcx_r = pltpu.roll(gcx_s, DN_DK, 1)
            gcol = jnp.where(lo, gcx_s, gcx_r) if j == 0 else jnp.where(lo, gcx_r, gcx_s)
            grow = gc_t[DN_HEADS * d + h:DN_HEADS * d + h + 1, :]
            diff = gcol - grow
            dec = jnp.where(incl, jnp.exp(jnp.where(incl, diff, 0.0)), 0.0)
            kk = _dot_nt(jnp.where(mine, kb_s, 0.0).astype(BF16), k_sb2)
            mats.append(jnp.where(strict, kk * dec, 0.0))
            qk = _dot_nt(jnp.where(mine, q_s, 0.0).astype(BF16), k_sb)
            attns.append((qk * dec[:, :c]).astype(BF16))
            if j == 0:
                rhss.append(jnp.where(lo, vb_s, pltpu.roll(wr_s, DN_DK, 1)))
            else:
                rhss.append(jnp.where(lo, pltpu.roll(vb_s, DN_DK, 1), wr_s))
        pairs.append(dict(mats=mats, rhss=rhss, attns=attns, qd=qd[:, sl].astype(BF16),
                          kd_t=kd[:, sl].T.astype(BF16), egl=egl_x[:, sl]))
    return pairs


def _dn_solve(mats, rhss):
    c = DN_CHUNK
    lo = lax.broadcasted_iota(jnp.int32, (1, 2 * c), 1) < c
    pw, xs = list(mats), list(rhss)
    rounds = 6
    for r in range(rounds):
        for n in range(len(pw)):
            ph = pw[n].astype(BF16)
            b = jnp.concatenate([xs[n], pw[n]], axis=1) if r < rounds - 1 else xs[n]
            bh = b.astype(BF16)
            if r < DN_EXACT_ROUNDS:
                plo = (pw[n] - ph.astype(F32)).astype(BF16)
                lhs = jnp.concatenate([jnp.where(lo, ph, plo), ph[:, :c]], axis=1)
                bl = (b - bh.astype(F32)).astype(BF16)
                both = _dot(lhs, jnp.concatenate([bh, bh, bl], axis=0))
            else:
                both = _dot(ph[:, :c], bh)
            px = both[:, :2 * c]
            if r < rounds - 1:
                pw[n] = both[:, 2 * c:]
            xs[n] = xs[n] - px if r == 0 else xs[n] + px
    return xs


def _dn_scan_kernel(alog_ref, dtb_ref, eg_ref, eb_ref,
                    qf, kf, vf, abf, qb, kb, vb, abb, of_ref, ob_ref, s_ref):
    i = pl.program_id(1)

    @pl.when(i == 0)
    def _():
        s_ref[...] = jnp.zeros_like(s_ref)

    alog = alog_ref[...]
    dtb = dtb_ref[...]
    nch = TM // DN_CHUNK
    npair = DN_HEADS // 2
    lane = lax.broadcasted_iota(jnp.int32, (1, LANE), 1)
    lo = lane < DN_DK
    ri2 = lax.broadcasted_iota(jnp.int32, (LANE, LANE), 0)
    ci2 = lax.broadcasted_iota(jnp.int32, (LANE, LANE), 1)
    bdiag = (ri2 < DN_DK) == (ci2 < DN_DK)

    def body(it, carry):
        pairs, rows_of = [], []
        for g in range(DN_GROUP):
            cidx = it * DN_GROUP + g
            rf = pl.ds(pl.multiple_of(cidx * DN_CHUNK, DN_CHUNK), DN_CHUNK)
            rb = pl.ds(pl.multiple_of((nch - 1 - cidx) * DN_CHUNK, DN_CHUNK), DN_CHUNK)
            pairs += (_dn_pre(0, rf, qf, kf, vf, abf, alog, dtb, eg_ref, eb_ref)
                      + _dn_pre(1, rb, qb, kb, vb, abb, alog, dtb, eg_ref, eb_ref))
            rows_of += [rf] * npair + [rb] * npair
        xs = _dn_solve([m for pr in pairs for m in pr["mats"]], [r for pr in pairs for r in pr["rhss"]])
        for m, pr in enumerate(pairs):
            n = m % (2 * npair)
            d, p = divmod(n, npair)
            x0, x1 = xs[2 * m], xs[2 * m + 1]
            u = jnp.where(lo, x0, pltpu.roll(x1, DN_DK, 1))
            w = jnp.where(lo, pltpu.roll(x0, DN_DK, 1), x1)
            s = s_ref[n]
            sb = s.astype(BF16)
            v_new = u - _dot(w.astype(BF16), sb)
            vn_b = v_new.astype(BF16)
            o = _dot(pr["qd"], sb)
            o = o + _dot(pr["attns"][0], jnp.where(lo, vn_b, jnp.zeros_like(vn_b)))
            o = o + _dot(pr["attns"][1], jnp.where(lo, jnp.zeros_like(vn_b), vn_b))
            upd = _dot(pr["kd_t"], vn_b)
            s_ref[n] = s * pr["egl"] + jnp.where(bdiag, upd, 0.0)
            o_ref = of_ref if d == 0 else ob_ref
            o_ref[rows_of[m], LANE * p:LANE * p + LANE] = o
        return carry

    lax.fori_loop(0, nch // DN_GROUP, body, 0)


def _dn_scan(q, k, v, ab, alog, dtb, eg, eb):
    nb, ntot, w = q.shape
    nt = ntot // TM
    nl = nt - 1
    fwd = lambda b, i: (b, jnp.where(i == 0, nl, i - 1), 0)
    bwd = lambda b, i: (b, jnp.where(i == 0, nl, nl - i), 0)
    const = lambda a: pl.BlockSpec(a.shape, lambda b, i: (0,) * a.ndim)
    blk = lambda ww, im: pl.BlockSpec((None, TM, ww), im)
    return pl.pallas_call(
        _dn_scan_kernel,
        grid=(nb, nt),
        in_specs=[const(alog), const(dtb), const(eg), const(eb),
                  blk(w, fwd), blk(w, fwd), blk(w, fwd), blk(LANE, fwd),
                  blk(w, bwd), blk(w, bwd), blk(w, bwd), blk(LANE, bwd)],
        out_specs=[blk(w, fwd), blk(w, bwd)],
        out_shape=[jax.ShapeDtypeStruct((nb, ntot, w), F32)] * 2,
        scratch_shapes=[pltpu.VMEM((2 * (DN_HEADS // 2), LANE, LANE), F32)],
        compiler_params=_params(("parallel", "arbitrary")),
        name="dn_scan",
    )(alog, dtb, eg, eb, q, k, v, ab, q, k, v, ab)


def _diff_attn_kernel(lam_init, q_ref, k_ref, vt_ref, lam_ref, g_ref, ones_ref, o_ref, st_ref):
    i = pl.program_id(1)
    nl = pl.num_programs(1) - 1
    lv = lam_ref[...]
    lam = (jnp.exp(jnp.sum(lv[0:1] * lv[1:2], axis=-1, keepdims=True))
           - jnp.exp(jnp.sum(lv[2:3] * lv[3:4], axis=-1, keepdims=True)) + lam_init)
    lane = lax.broadcasted_iota(jnp.int32, (1, B_QK_W), 1)

    nsm = 2 * DA_HEADS

    def run(groups):
        q = q_ref[...]
        m_prev = None
        res = []
        for n in range(nsm + 1):
            if n < nsm:
                slot = slice(DA_QK * n // LANE * LANE, DA_QK * n // LANE * LANE + LANE)
                lo = DA_QK * n
                qm = jnp.where(jnp.logical_and(lane >= lo, lane < lo + DA_QK), q, jnp.zeros_like(q))[:, slot]
            hp = (n - 1) // 2
            m8 = jnp.full((8, TM), -jnp.inf, F32)
            acc = jnp.zeros((DA_V + 16, TM), F32)
            for grp in groups:
                rows = slice(grp[0] * TM, (grp[-1] + 1) * TM)
                nk = len(grp) * TM
                if n < nsm:
                    st = _dot_nt(k_ref[rows, slot], qm)
                    st_ref[n % 2, rows, :] = st
                    m8 = jnp.maximum(m8, jnp.max(st.reshape(nk // 8, 8, TM), axis=0))
                if n > 0:
                    e = jnp.exp2(st_ref[(n - 1) % 2, rows, :] - m_prev).astype(BF16)
                    vt = jnp.concatenate([vt_ref[c, DA_V * hp:DA_V * hp + DA_V, :] for c in grp], axis=1)
                    lhs = jnp.concatenate([vt, jnp.ones((16, nk), BF16)], axis=0)
                    acc = acc + _dot(lhs, e)
            if n > 0:
                res.append(acc[:DA_V] / acc[DA_V:DA_V + 1])
            if n < nsm:
                m_prev = jnp.max(m8, axis=0, keepdims=True)
        ot = jnp.concatenate([res[2 * h] - lam * res[2 * h + 1] for h in range(DA_HEADS)], axis=0)
        ms = _dot_exact_lhs(ones_ref[...], ot * ot, parts=2)
        yt = (ot * lax.rsqrt(ms + EPS)) * g_ref[...]
        o_ref[...] = (yt * (1.0 - lam_init)).T.astype(BF16)

    ntiles = st_ref.shape[1] // TM

    @pl.when(i < nl)
    def _():
        run([tuple(range(c, min(c + DA_KEY_GROUP, ntiles))) for c in range(0, ntiles, DA_KEY_GROUP)])

    @pl.when(i == nl)
    def _():
        run([(ntiles - 1,)])


def _diff_attn(q, k, vt, lam_vecs, subln_g, ones_b, lam_init):
    nb, ntot, w = q.shape
    nt = ntot // TM
    row = pl.BlockSpec((None, TM, w), lambda b, i: (b, i, 0))
    full = pl.BlockSpec((None, ntot, w), lambda b, i: (b, 0, 0))
    full_t = pl.BlockSpec((None,) + vt.shape[1:], lambda b, i: (b, 0, 0, 0))
    const = lambda a: pl.BlockSpec(a.shape, lambda b, i: (0,) * a.ndim)
    return pl.pallas_call(
        functools.partial(_diff_attn_kernel, lam_init),
        grid=(nb, nt),
        in_specs=[row, full, full_t, const(lam_vecs), const(subln_g), const(ones_b)],
        out_specs=row,
        out_shape=jax.ShapeDtypeStruct((nb, ntot, w), BF16),
        scratch_shapes=[pltpu.VMEM((2, ntot, TM), F32)],
        compiler_params=_params(("parallel", "arbitrary")),
        name="diff_attn",
    )(q, k, vt, lam_vecs, subln_g, ones_b)


def _win_attn_kernel(n_lat, sink_ref, q_ref, k_ref, v_ref, o_ref):
    i = pl.program_id(1)
    nl = pl.num_programs(1) - 1
    rep = WA_HEADS // WA_KV_HEADS
    lane = lax.broadcasted_iota(jnp.int32, (1, LANE), 1)
    lo = lane < WA_DIM
    rowg = lax.broadcasted_iota(jnp.int32, (rep * TM, 1), 0) // TM

    def run(k_all, v_all, bias):
        outs = []
        for g in range(WA_KV_HEADS):
            mine = lo if g == 0 else jnp.logical_not(lo)
            q3 = jnp.concatenate(
                [jnp.where(mine, q_ref[:, LANE * s:LANE * s + LANE], jnp.zeros((TM, LANE), BF16))
                 for s in range(rep)], axis=0)
            s = _dot_nt(q3, k_all)
            if bias is not None:
                s = s + bias
            sk = jnp.zeros((rep * TM, 1), F32)
            for r in range(rep):
                sk = jnp.where(rowg == r, sink_ref[rep * g + r], sk)
            m = jnp.maximum(jnp.max(s, axis=-1, keepdims=True), sk)
            e = jnp.exp(s - m)
            den = jnp.sum(e, axis=-1, keepdims=True) + jnp.exp(sk - m)
            outs.append(_dot(e.astype(BF16), v_all) / den)
        for s in range(rep):
            o_ref[:, LANE * s:LANE * s + LANE] = jnp.where(
                lo, outs[0][TM * s:TM * s + TM], outs[1][TM * s:TM * s + TM]).astype(BF16)

    kc = k_ref[n_lat:, :]
    vc = v_ref[n_lat:, :]
    band = TM + 2 * WINDOW

    @pl.when(i < nl)
    def _():
        start = pl.multiple_of(jnp.clip(i * TM - WINDOW, 0, n_lat - band), WINDOW)
        kb = k_ref[pl.ds(start, band), :]
        vb = v_ref[pl.ds(start, band), :]
        qpos = i * TM + lax.broadcasted_iota(jnp.int32, (TM, 1), 0)
        kpos = start + lax.broadcasted_iota(jnp.int32, (1, band), 1)
        near = jnp.where(jnp.abs(qpos - kpos) <= WINDOW, 0.0, -1e30)
        bias = jnp.concatenate([near, jnp.zeros((TM, kc.shape[0]), F32)], axis=1)
        bias = jnp.concatenate([bias] * rep, axis=0)
        run(jnp.concatenate([kb, kc], axis=0), jnp.concatenate([vb, vc], axis=0), bias)

    @pl.when(i == nl)
    def _():
        run(kc, vc, None)


def _win_attn(q, k, v, sink, n_lat):
    nb, ntot, w = q.shape
    nt = ntot // TM
    kw = k.shape[-1]
    row = pl.BlockSpec((None, TM, w), lambda b, i: (b, i, 0))
    full = pl.BlockSpec((None, ntot, kw), lambda b, i: (b, 0, 0))
    return pl.pallas_call(
        functools.partial(_win_attn_kernel, n_lat),
        grid=(nb, nt),
        in_specs=[pl.BlockSpec(memory_space=pltpu.SMEM), row, full, full],
        out_specs=row,
        out_shape=jax.ShapeDtypeStruct((nb, ntot, w), BF16),
        compiler_params=_params(("parallel", "arbitrary")),
        name="win_attn",
    )(sink, q, k, v)


def _out_proj_kernel(x_ref, of_ref, ob_ref, gate_ref, yb_ref, yw_ref, wa_ref, wb_ref, wc_ref, dng_ref,
                     ones_ref, g1_ref, sc2_ref, sh2_ref, n2_ref, wr_ref, br_ref, xn_o, h2_o, lg_o):
    o = of_ref[...] + ob_ref[...]
    ms = _dot_exact_rhs(o * o, ones_ref[...], parts=2)
    ya = ((o * lax.rsqrt(ms + EPS)) * dng_ref[...]) * _silu(gate_ref[...])
    y = _dot(ya.astype(BF16), wa_ref[...]) + _dot(yb_ref[...], wb_ref[...]) + _dot(yw_ref[...], wc_ref[...])
    xn = x_ref[...] + g1_ref[...] * y
    xn_o[...] = xn
    h2 = _rms_mod(xn, n2_ref[...], sc2_ref[...], sh2_ref[...])
    h2_o[...] = h2.astype(BF16)
    lg_o[...] = _dot3(h2, wr_ref[...]) + br_ref[...]


def _out_proj(x, of, ob, gate, yb, yw, wa, wb, wc, dng, ones_a, mod, n2g, wr, br, nb, ntot):
    d = x.shape[-1]
    nt = ntot // TM
    nl = nt - 1
    row = lambda w: pl.BlockSpec((None, TM, w), lambda b, i: (b, i, 0))
    const = lambda a: pl.BlockSpec(a.shape, lambda b, i: (0,) * a.ndim)
    return pl.pallas_call(
        _out_proj_kernel,
        grid=(nb, nt),
        in_specs=[row(d), row(A_W), row(A_W), row(A_W), row(B_W), row(C_W),
                  const(wa), const(wb), const(wc), const(dng), const(ones_a),
                  _mod_spec_d(2, nb, nl, d), _mod_spec_d(4, nb, nl, d), _mod_spec_d(3, nb, nl, d),
                  const(n2g), const(wr), const(br)],
        out_specs=[row(d), row(d), row(LANE)],
        out_shape=[jax.ShapeDtypeStruct((nb, ntot, d), F32), jax.ShapeDtypeStruct((nb, ntot, d), BF16),
                   jax.ShapeDtypeStruct((nb, ntot, LANE), F32)],
        compiler_params=_params(("parallel", "arbitrary")),
        name="out_proj",
    )(x, of, ob, gate, yb, yw, wa, wb, wc, dng, ones_a, mod, mod, mod, n2g, wr, br)


def _route_kernel(lg_ref, gw_o, pos_o, cnt_o):
    t = lg_ref.shape[0]
    lg = lg_ref[...]
    lane_i = lax.broadcasted_iota(jnp.int32, (1, LANE), 1)
    lane = lane_i.astype(F32)
    big = float(LANE)
    neg = -jnp.inf
    is_g = lane_i < N_GROUPS
    lgm = jnp.where(is_g, lg, neg)
    mg = jnp.max(lgm, axis=-1, keepdims=True)
    p_sel = 1.0 / jnp.sum(jnp.where(is_g, jnp.exp(lgm - mg), 0.0), axis=-1, keepdims=True)
    gidx = jnp.min(jnp.where(jnp.logical_and(is_g, lgm == mg), lane, big), axis=-1, keepdims=True)
    e_lane = lane_i - N_GROUPS
    in_grp = jnp.logical_and(jnp.logical_and(e_lane >= 0, e_lane < N_EXPERTS),
                             jnp.floor((lane - N_GROUPS) * (1.0 / EXP_PER_GROUP)) == gidx)
    le = jnp.where(in_grp, lg, neg)
    v1 = jnp.max(le, axis=-1, keepdims=True)
    i1 = jnp.min(jnp.where(jnp.logical_and(in_grp, le == v1), lane, big), axis=-1, keepdims=True)
    is1 = lane == i1
    le2 = jnp.where(is1, neg, le)
    v2 = jnp.max(le2, axis=-1, keepdims=True)
    rest = jnp.logical_and(in_grp, jnp.logical_not(is1))
    i2 = jnp.min(jnp.where(jnp.logical_and(rest, le2 == v2), lane, big), axis=-1, keepdims=True)
    is2 = lane == i2
    e2 = jnp.exp(v2 - v1)
    w1 = 1.0 / (1.0 + e2)
    w2 = e2 / (1.0 + e2)
    gw = jnp.where(is1, p_sel * w1, jnp.where(is2, p_sel * w2, 0.0))
    sel = jnp.logical_or(is1, is2)
    self_ = jnp.where(sel, 1.0, 0.0)
    ri = lax.broadcasted_iota(jnp.int32, (ROUTE_BLK, ROUTE_BLK), 0)
    ci = lax.broadcasted_iota(jnp.int32, (ROUTE_BLK, ROUTE_BLK), 1)
    tri = jnp.where(ri > ci, 1.0, 0.0).astype(BF16)
    run = jnp.zeros((1, LANE), F32)
    pos_blocks = []
    for b in range(t // ROUTE_BLK):
        blk = self_[b * ROUTE_BLK:(b + 1) * ROUTE_BLK]
        pos_blocks.append(_dot(tri, blk.astype(BF16)) + run)
        run = run + jnp.sum(blk, axis=0, keepdims=True)
    pos = jnp.where(sel, jnp.concatenate(pos_blocks, axis=0), -1.0)
    gw_o[...] = gw.T
    pos_o[...] = pos.T
    cnt_o[...] = jnp.broadcast_to(run, (8, LANE)).astype(jnp.int32)


def _route(logits):
    ntok = logits.shape[0]
    ntile = ntok // MOE_T
    return pl.pallas_call(
        _route_kernel,
        grid=(ntile,),
        in_specs=[pl.BlockSpec((MOE_T, LANE), lambda t: (t, 0))],
        out_specs=[pl.BlockSpec((None, LANE, MOE_T), lambda t: (t, 0, 0)),
                   pl.BlockSpec((None, LANE, MOE_T), lambda t: (t, 0, 0)),
                   pl.BlockSpec((None, 8, LANE), lambda t: (t, 0, 0))],
        out_shape=[jax.ShapeDtypeStruct((ntile, LANE, MOE_T), F32),
                   jax.ShapeDtypeStruct((ntile, LANE, MOE_T), F32),
                   jax.ShapeDtypeStruct((ntile, 8, LANE), jnp.int32)],
        compiler_params=_params(("parallel",)),
        name="moe_route",
    )(logits)


def _moe_kernel(cnt_ref, h_ref, pos_ref, gw_ref, wg_ref, wu_ref, wd_ref, o_ref):
    t = pl.program_id(0)
    e = pl.program_id(1)

    @pl.when(e == 0)
    def _():
        o_ref[...] = jnp.zeros_like(o_ref)

    n = cnt_ref[t * N_EXPERTS + e]
    prow = pos_ref[pl.ds(N_GROUPS + e, 1), :]
    grow = gw_ref[pl.ds(N_GROUPS + e, 1), :]

    def chunk(c, carry):
        slot = (lax.broadcasted_iota(jnp.int32, (MOE_CH, 1), 0) + c * MOE_CH).astype(F32)
        hit = prow == slot
        onehot = jnp.where(hit, 1.0, 0.0).astype(BF16)
        hc = _dot(onehot, h_ref[...]).astype(BF16)
        a = _dot(hc, wg_ref[...])
        b = _dot(hc, wu_ref[...])
        y = _dot((_silu(a) * b).astype(BF16), wd_ref[...])
        gcol = jnp.sum(jnp.where(hit, grow, 0.0), axis=-1, keepdims=True)
        o_ref[...] += _dot_tn(onehot, (y * gcol).astype(BF16))
        return carry

    lax.fori_loop(0, (n + MOE_CH - 1) // MOE_CH, chunk, 0)


def _moe(counts, h2, pos_t, gw_t, wg, wu, wd):
    ntok, d = h2.shape
    ntile = ntok // MOE_T
    gs = pltpu.PrefetchScalarGridSpec(
        num_scalar_prefetch=1,
        grid=(ntile, N_EXPERTS),
        in_specs=[pl.BlockSpec((MOE_T, d), lambda t, e, c: (t, 0)),
                  pl.BlockSpec((None, LANE, MOE_T), lambda t, e, c: (t, 0, 0)),
                  pl.BlockSpec((None, LANE, MOE_T), lambda t, e, c: (t, 0, 0)),
                  pl.BlockSpec((None, d, D_EXPERT), lambda t, e, c: (e, 0, 0)),
                  pl.BlockSpec((None, d, D_EXPERT), lambda t, e, c: (e, 0, 0)),
                  pl.BlockSpec((None, D_EXPERT, d), lambda t, e, c: (e, 0, 0))],
        out_specs=pl.BlockSpec((MOE_T, d), lambda t, e, c: (t, 0)),
    )
    return pl.pallas_call(
        _moe_kernel,
        grid_spec=gs,
        out_shape=jax.ShapeDtypeStruct((ntok, d), F32),
        compiler_params=_params(("parallel", "arbitrary")),
        name="moe_experts",
    )(counts, h2, pos_t, gw_t, wg, wu, wd)


def _final_kernel(xn_ref, ff_ref, g2_ref, g_ref, o_ref):
    x = xn_ref[...] + g2_ref[...] * ff_ref[...]
    y = x * lax.rsqrt(jnp.mean(x * x, axis=-1, keepdims=True) + EPS)
    o_ref[...] = y * g_ref[...]


def _final(xn, ff, mod, g, nb, n_lat, ntot):
    d = xn.shape[-1]
    nl = n_lat // TM
    row = pl.BlockSpec((None, TM, d), lambda b, i: (b, i, 0))
    return pl.pallas_call(
        _final_kernel,
        grid=(nb, nl),
        in_specs=[row, row, _mod_spec_d(5, nb, nl, d), pl.BlockSpec(g.shape, lambda b, i: (0, 0))],
        out_specs=row,
        out_shape=jax.ShapeDtypeStruct((nb, n_lat, d), F32),
        compiler_params=_params(("parallel", "arbitrary")),
        name="final_norm",
    )(xn, ff, mod, g)


def _rope_swap_perm(width, dim):
    nf = dim // 4
    j = jnp.arange(width)
    base = (j // (2 * nf)) * (2 * nf)
    return base + (j % (2 * nf) + nf) % (2 * nf)


def _rope_tables(n_lat, n_ctx, dim, width):
    nf = dim // 4
    t = jnp.arange(n_lat)
    row = (t // GRID_W).astype(F32)
    col = (t % GRID_W).astype(F32)
    inv = ROPE_BASE ** (-jnp.arange(nf, dtype=F32) / nf)
    j = jnp.arange(width) % dim
    axis = j // (2 * nf)
    pos = jnp.where(axis[None, :] == 0, row[:, None], col[:, None])
    ang = pos * inv[j % nf][None, :]
    sign = jnp.where(j % (2 * nf) < nf, -1.0, 1.0).astype(F32)
    cos = jnp.concatenate([jnp.cos(ang), jnp.ones((n_ctx, width), F32)], axis=0)
    sin = jnp.concatenate([jnp.sin(ang) * sign[None, :], jnp.zeros((n_ctx, width), F32)], axis=0)
    return cos, sin


def _block_ones(width, group, value):
    j = jnp.arange(width)
    return jnp.where((j[:, None] // group) == (j[None, :] // group), value, 0.0).astype(BF16)


def _slot_cols():
    rep = WA_HEADS // WA_KV_HEADS
    cols = []
    for s in range(rep):
        cols.append(jnp.arange(WA_DIM) + WA_DIM * s)
        cols.append(jnp.arange(WA_DIM) + WA_DIM * (rep + s))
    return jnp.concatenate(cols)


def _build_w_in(w):
    d = w.shape[0]
    s1, s2 = A_COLS, A_COLS + B_COLS
    wa, wb, wc = w[:, :s1], w[:, s1:s2], w[:, s2:]
    ab = jnp.pad(wa[:, QKV_W + A_W:], ((0, 0), (0, LANE - 4 * DN_HEADS)))
    bq, bk, bv = wb[:, :B_QK_W], wb[:, B_QK_W:2 * B_QK_W], wb[:, 2 * B_QK_W:]
    pb = _rope_swap_perm(B_QK_W, DA_QK)
    cq = wc[:, :C_W][:, _slot_cols()]
    ck, cv = wc[:, C_W:C_W + C_KV_W], wc[:, C_W + C_KV_W:]
    pcq = _rope_swap_perm(C_W, WA_DIM)
    pck = _rope_swap_perm(C_KV_W, WA_DIM)
    cat = jnp.concatenate([wa[:, :QKV_W], wa[:, QKV_W:QKV_W + A_W], ab,
                           bq, bk, bv, bq[:, pb], bk[:, pb],
                           cq, ck, cv, cq[:, pcq], ck[:, pck]], axis=1)
    assert cat.shape == (d, _O_END)
    return cat.astype(BF16)


def _expand_mats():
    r = jnp.arange(LANE)[:, None]
    h = (jnp.arange(A_QK_W) // DN_DK)[None, :]
    eg = jnp.stack([(r == DN_HEADS * d + h) for d in range(2)]).astype(BF16)
    eb = jnp.stack([(r == 2 * DN_HEADS + DN_HEADS * d + h) for d in range(2)]).astype(BF16)
    return eg, eb


def kernel(x, c, ctx, c_ctx, ada_w, ada_b, norm1_g, norm2_g, w_in, dn_conv_w, dn_a_log, dn_dt_bias, dn_norm_g, da_lambda, da_subln_g, wa_sink, w_out, router_group_w, router_group_b, router_expert_w, router_expert_b, exp_w_gate, exp_w_up, exp_w_down, final_norm_g):
    nb, n_lat, d = x.shape
    n_ctx = ctx.shape[1]
    depth = ada_w.shape[0]
    assert n_ctx == TM and n_lat % TM == 0 and n_lat >= 3 * TM
    ntot = n_lat + n_ctx
    ntok = nb * ntot
    assert ntok % MOE_T == 0

    xa = jnp.concatenate([x, ctx], axis=1)
    cc = jnp.zeros((16, d), F32).at[:nb].set(c).at[nb].set(c_ctx)
    cosb, sinb = _rope_tables(n_lat, n_ctx, DA_QK, B_QK_W)
    cosc, sinc = _rope_tables(n_lat, n_ctx, WA_DIM, C_KV_W)
    tabs = (cosb, sinb, cosc, sinc)
    ones_a = _block_ones(A_W, DN_DV, 1.0)
    mean_a = _block_ones(A_W, DN_DV, 1.0 / DN_DV)
    mean_b = _block_ones(B_W, DA_V, 1.0 / DA_V)
    eg, eb = _expand_mats()
    slot_rows = _slot_cols()
    pad_lane = lambda v: jnp.pad(v.reshape(1, -1), ((0, 0), (0, LANE - v.size)))

    mods = [_ada(cc, ada_w[li], ada_b[li]).reshape(16, 6, 1, d) for li in range(depth)]
    xn = ff = None
    for li in range(depth):
        mod = mods[li]
        w_cat = _build_w_in(w_in[li])
        n1g = norm1_g[li].reshape(1, d)
        if li == 0:
            outs = _in_proj((xa,), (mod,), n1g, w_cat, tabs, nb, ntot)
            xcur = xa
        else:
            outs = _in_proj((xn, ff.reshape(nb, ntot, d)), (mod, mods[li - 1]), n1g, w_cat, tabs, nb, ntot)
            xcur, outs = outs[0], outs[1:]
        zqkv, gate, ab, qb, kb, vb, qc, kc, vc = outs

        conv_w8 = jnp.pad(dn_conv_w[li], ((0, 8 - DN_CONV), (0, 0)))
        q, k, v = _dn_prep(zqkv, conv_w8, ones_a)
        of, ob = _dn_scan(q, k, v, ab, pad_lane(dn_a_log[li]), pad_lane(dn_dt_bias[li]), eg, eb)

        lam_init = 0.8 - 0.6 * math.exp(-0.3 * li)
        yb = _diff_attn(qb, kb, vb, da_lambda[li], jnp.tile(da_subln_g[li], DA_HEADS).reshape(B_W, 1),
                        mean_b, lam_init)
        yw = _win_attn(qc, kc, vc, jnp.pad(wa_sink[li], (0, 8 - WA_HEADS)), n_lat)

        wo = w_out[li]
        wa_o = wo[:A_W].astype(BF16)
        wb_o = wo[A_W:A_W + B_W].astype(BF16)
        wc_o = wo[A_W + B_W:][slot_rows].astype(BF16)
        wr = jnp.pad(jnp.concatenate([router_group_w[li], router_expert_w[li]], axis=1),
                     ((0, 0), (0, LANE - N_GROUPS - N_EXPERTS)))
        br = pad_lane(jnp.concatenate([router_group_b[li], router_expert_b[li]]))
        xn, h2, logits = _out_proj(xcur, of, ob, gate, yb, yw, wa_o, wb_o, wc_o,
                                   jnp.tile(dn_norm_g[li], DN_HEADS).reshape(1, A_W), mean_a, mod,
                                   norm2_g[li].reshape(1, d), wr, br, nb, ntot)

        gw_t, pos_t, cnt = _route(logits.reshape(ntok, LANE))
        counts = cnt[:, 0, N_GROUPS:N_GROUPS + N_EXPERTS].reshape(-1)
        ff = _moe(counts, h2.reshape(ntok, d), pos_t, gw_t, exp_w_gate[li].astype(BF16),
                  exp_w_up[li].astype(BF16), exp_w_down[li].astype(BF16))

    return _final(xn, ff.reshape(nb, ntot, d), mods[depth - 1], final_norm_g.reshape(1, d), nb, n_lat, ntot)
```

```python
import functools
import math

import jax
import jax.numpy as jnp
from jax import lax
from jax.experimental import pallas as pl
from jax.experimental.pallas import tpu as pltpu

F32 = jnp.float32
BF16 = jnp.bfloat16

GRID_W = 64
EPS = 1e-6
LOG2E = math.log2(math.e)
ROPE_BASE = 10000.0
DN_HEADS = 6
DN_DK = 64
DN_DV = 64
DN_CONV = 5
DN_CHUNK = 64
DA_HEADS = 4
DA_QK = 32
DA_V = 64
WA_HEADS = 6
WA_KV_HEADS = 2
WA_DIM = 64
WINDOW = 128
N_GROUPS = 4
EXP_PER_GROUP = 4
N_EXPERTS = 16
D_EXPERT = 512

A_QK_W = DN_HEADS * DN_DK
A_W = DN_HEADS * DN_DV
QKV_W = 2 * A_QK_W + A_W
B_W = DA_HEADS * DA_V
B_QK_W = 2 * DA_HEADS * DA_QK
C_W = WA_HEADS * WA_DIM
C_KV_W = WA_KV_HEADS * WA_DIM
A_COLS = QKV_W + A_W + 4 * DN_HEADS
B_COLS = 2 * B_QK_W + B_W

LANE = 128
TM = 256
MOE_T = 1024
MOE_CH = 288
ROUTE_BLK = 256
DN_EXACT_ROUNDS = 5
DA_KEY_GROUP = 4
DN_GROUP = 4
VMEM_LIMIT = 56 * 1024 * 1024

_O_QKV = 0
_O_GATE = _O_QKV + QKV_W
_O_AB = _O_GATE + A_W
_O_BQ = _O_AB + LANE
_O_BK = _O_BQ + B_QK_W
_O_BV = _O_BK + B_QK_W
_O_BQS = _O_BV + B_W
_O_BKS = _O_BQS + B_QK_W
_O_CQ = _O_BKS + B_QK_W
_O_CK = _O_CQ + C_W
_O_CV = _O_CK + C_KV_W
_O_CQS = _O_CV + C_KV_W
_O_CKS = _O_CQS + C_W
_O_END = _O_CKS + C_KV_W


def _dot(a, b):
    return jnp.dot(a, b, preferred_element_type=F32)


def _dot_nt(a, b):
    return lax.dot_general(a, b, (((1,), (1,)), ((), ())), preferred_element_type=F32)


def _dot_tn(a, b):
    return lax.dot_general(a, b, (((0,), (0,)), ((), ())), preferred_element_type=F32)


def _split2(a):
    hi = a.astype(BF16)
    lo = (a - hi.astype(F32)).astype(BF16)
    return hi, lo


def _split3(a):
    hi = a.astype(BF16)
    r = a - hi.astype(F32)
    mid = r.astype(BF16)
    lo = (r - mid.astype(F32)).astype(BF16)
    return hi, mid, lo


def _dot3(a, b):
    ah, al = _split2(a)
    bh, bl = _split2(b)
    return _dot(ah, bh) + (_dot(ah, bl) + _dot(al, bh))


def _dot_exact_rhs(a, b_bf16, parts=3):
    sp = _split3(a) if parts == 3 else _split2(a)
    out = _dot(sp[0], b_bf16)
    for p in sp[1:]:
        out = out + _dot(p, b_bf16)
    return out


def _dot_exact_lhs(a_bf16, b, parts=3):
    sp = _split3(b) if parts == 3 else _split2(b)
    out = _dot(a_bf16, sp[0])
    for p in sp[1:]:
        out = out + _dot(a_bf16, p)
    return out


def _col_reduce(x, op, slab=64):
    n, w = x.shape
    if n > slab and n % slab == 0:
        x = op(x.reshape(n // slab, slab, w), axis=0)
    return op(x, axis=0, keepdims=True)


def _silu(x):
    return x * jax.nn.sigmoid(x)


def _softplus(x):
    return jnp.maximum(x, 0.0) + jnp.log1p(jnp.exp(-jnp.abs(x)))


def _params(sem):
    return pltpu.CompilerParams(dimension_semantics=sem, vmem_limit_bytes=VMEM_LIMIT)


def _ada_kernel(c_ref, w_ref, b_ref, o_ref):
    o_ref[...] = _dot3(_silu(c_ref[...]), w_ref[...]) + b_ref[...]


def _ada(cc, w, b):
    rows, d = cc.shape
    n = w.shape[1]
    tn = n // 4
    return pl.pallas_call(
        _ada_kernel,
        grid=(n // tn,),
        in_specs=[pl.BlockSpec((rows, d), lambda j: (0, 0)),
                  pl.BlockSpec((d, tn), lambda j: (0, j)),
                  pl.BlockSpec((1, tn), lambda j: (0, j))],
        out_specs=pl.BlockSpec((rows, tn), lambda j: (0, j)),
        out_shape=jax.ShapeDtypeStruct((rows, n), F32),
        compiler_params=_params(("arbitrary",)),
        name="ada_mod",
    )(cc, w, b.reshape(1, n))


def _mod_spec_d(k, nb, nl, d):
    return pl.BlockSpec((None, None, 1, d), lambda b, i: (jnp.where(i == nl, nb, b), k, 0, 0))


def _rms_mod(x, g, sc, sh):
    y = x * lax.rsqrt(jnp.mean(x * x, axis=-1, keepdims=True) + EPS)
    return (y * g) * (1.0 + sc) + sh


def _in_proj_kernel(fuse_res, *refs):
    if fuse_res:
        xn_ref, ff_ref, g2_ref = refs[:3]
        refs = refs[3:]
    else:
        x_ref = refs[0]
        refs = refs[1:]
    (sc_ref, sh_ref, g_ref, w_ref, cosb_ref, sinb_ref, cosc_ref, sinc_ref) = refs[:8]
    outs = refs[8:]
    if fuse_res:
        x = xn_ref[...] + g2_ref[...] * ff_ref[...]
        outs[0][...] = x
        outs = outs[1:]
    else:
        x = x_ref[...]
    (zqkv_o, gate_o, ab_o, qb_o, kb_o, vb_o, qc_o, kc_o, vc_o) = outs
    hb = _rms_mod(x, g_ref[...], sc_ref[...], sh_ref[...]).astype(BF16)

    def seg(a, b):
        return _dot(hb, w_ref[:, a:b])

    zqkv_o[...] = seg(_O_QKV, _O_GATE)
    gate_o[...] = seg(_O_GATE, _O_AB)
    ab_o[...] = seg(_O_AB, _O_BQ)
    cb = cosb_ref[...]
    sb = sinb_ref[...]
    qb_o[...] = ((seg(_O_BQ, _O_BK) * cb + seg(_O_BQS, _O_BKS) * sb) * (DA_QK ** -0.5 * LOG2E)).astype(BF16)
    kb_o[...] = (seg(_O_BK, _O_BV) * cb + seg(_O_BKS, _O_CQ) * sb).astype(BF16)
    vb_o[...] = seg(_O_BV, _O_BQS).T.astype(BF16)
    cc = cosc_ref[...]
    sc_ = sinc_ref[...]
    cc3 = jnp.concatenate([cc, cc, cc], axis=1)
    sc3 = jnp.concatenate([sc_, sc_, sc_], axis=1)
    qc_o[...] = ((seg(_O_CQ, _O_CK) * cc3 + seg(_O_CQS, _O_CKS) * sc3) * (WA_DIM ** -0.5)).astype(BF16)
    kc_o[...] = (seg(_O_CK, _O_CV) * cc + seg(_O_CKS, _O_END) * sc_).astype(BF16)
    vc_o[...] = seg(_O_CV, _O_CQS).astype(BF16)


def _in_proj(x_parts, mod, norm_g, w_cat, tabs, nb, ntot):
    fuse_res = len(x_parts) == 2
    d = x_parts[0].shape[-1]
    nt = ntot // TM
    nl = nt - 1
    row = lambda w: pl.BlockSpec((None, TM, w), lambda b, i: (b, i, 0))
    tab = lambda w: pl.BlockSpec((TM, w), lambda b, i: (i, 0))
    const = lambda a: pl.BlockSpec(a.shape, lambda b, i: (0,) * a.ndim)
    if fuse_res:
        (xn, ff), prev_mod = x_parts, mod[1]
        ins = [xn, ff, prev_mod]
        in_specs = [row(d), row(d), _mod_spec_d(5, nb, nl, d)]
        cur_mod = mod[0]
    else:
        ins = [x_parts[0]]
        in_specs = [row(d)]
        cur_mod = mod[0]
    ins += [cur_mod, cur_mod, norm_g, w_cat, *tabs]
    in_specs += [_mod_spec_d(1, nb, nl, d), _mod_spec_d(0, nb, nl, d), const(norm_g), const(w_cat),
                 tab(B_QK_W), tab(B_QK_W), tab(C_KV_W), tab(C_KV_W)]
    widths = [(QKV_W, F32), (A_W, F32), (LANE, F32), (B_QK_W, BF16), (B_QK_W, BF16), (None, BF16),
              (C_W, BF16), (C_KV_W, BF16), (C_KV_W, BF16)]
    if fuse_res:
        widths = [(d, F32)] + widths
    return pl.pallas_call(
        functools.partial(_in_proj_kernel, fuse_res),
        grid=(nb, nt),
        in_specs=in_specs,
        out_specs=[pl.BlockSpec((None, None, B_W, TM), lambda b, i: (b, i, 0, 0)) if w is None else row(w)
                   for w, _ in widths],
        out_shape=[jax.ShapeDtypeStruct((nb, nt, B_W, TM) if w is None else (nb, ntot, w), dt)
                   for w, dt in widths],
        compiler_params=_params(("parallel", "arbitrary")),
        name="in_proj",
    )(*ins)


def _dn_prep_kernel(zc_ref, zp_ref, zn_ref, w_ref, ones_ref, q_o, k_o, v_o, ext_ref):
    i = pl.program_id(1)
    nl = pl.num_programs(1) - 1
    prev_ok = jnp.logical_and(i >= 1, i <= nl - 1)
    next_ok = i <= nl - 2
    ext_ref[0:8, :] = jnp.where(prev_ok, zp_ref[...], 0.0)
    ext_ref[8:8 + TM, :] = zc_ref[...]
    ext_ref[8 + TM:16 + TM, :] = jnp.where(next_ok, zn_ref[...], 0.0)
    half = DN_CONV // 2
    acc = w_ref[0:1, :] * ext_ref[8 - half:8 - half + TM, :]
    for j in range(1, DN_CONV):
        acc = acc + w_ref[j:j + 1, :] * ext_ref[8 - half + j:8 - half + j + TM, :]
    y = _silu(acc)
    ones = ones_ref[...]

    def l2n(t):
        ss = _dot_exact_rhs(t * t, ones, parts=2)
        return t * lax.rsqrt(ss + EPS)

    q_o[...] = l2n(y[:, :A_QK_W]) * (DN_DK ** -0.5)
    k_o[...] = l2n(y[:, A_QK_W:2 * A_QK_W])
    v_o[...] = y[:, 2 * A_QK_W:]


def _dn_prep(zqkv, conv_w8, ones_a):
    nb, ntot, w = zqkv.shape
    nt = ntot // TM
    r8 = TM // 8
    row = lambda ww: pl.BlockSpec((None, TM, ww), lambda b, i: (b, i, 0))
    return pl.pallas_call(
        _dn_prep_kernel,
        grid=(nb, nt),
        in_specs=[row(w),
                  pl.BlockSpec((None, 8, w), lambda b, i: (b, jnp.maximum(i * r8 - 1, 0), 0)),
                  pl.BlockSpec((None, 8, w), lambda b, i: (b, jnp.minimum(i * r8 + r8, ntot // 8 - 1), 0)),
                  pl.BlockSpec(conv_w8.shape, lambda b, i: (0, 0)),
                  pl.BlockSpec(ones_a.shape, lambda b, i: (0, 0))],
        out_specs=[row(A_QK_W), row(A_QK_W), row(A_W)],
        out_shape=[jax.ShapeDtypeStruct((nb, ntot, A_QK_W), F32)] * 3,
        scratch_shapes=[pltpu.VMEM((TM + 16, w), F32)],
        compiler_params=_params(("parallel", "arbitrary")),
        name="dn_prep",
    )(zqkv, zqkv, zqkv, conv_w8, ones_a)


def _dn_pre(d, rows, q_ref, k_ref, v_ref, ab_ref, alog, dtb, eg_ref, eb_ref):
    c = DN_CHUNK
    q = q_ref[rows, :]
    k = k_ref[rows, :]
    v = v_ref[rows, :]
    ab = ab_ref[rows, :]
    g = -jnp.exp(alog) * _softplus(ab + dtb)
    beta = jax.nn.sigmoid(ab)
    ri = lax.broadcasted_iota(jnp.int32, (c, 2 * c), 0)
    ci = lax.broadcasted_iota(jnp.int32, (c, 2 * c), 1) % c
    if d == 0:
        incl, strict = ri >= ci, ri > ci
    else:
        incl, strict = ri <= ci, ri < ci
    cum = jnp.where(incl[:, :c], 1.0, 0.0).astype(BF16)
    gc = _dot_exact_lhs(cum, g)
    last = c - 1 if d == 0 else 0
    g_last = gc[last:last + 1, :]
    egc = jnp.exp(gc)
    ekd = jnp.exp(g_last - gc)
    egl = jnp.broadcast_to(jnp.exp(g_last), (8, LANE))
    eg = eg_ref[d]
    eb = eb_ref[d]
    beta_x = _dot_exact_rhs(beta, eb)
    gx = _dot_exact_rhs(jnp.concatenate([egc, ekd, gc, egl], axis=0), eg)
    egc_x, ekd_x, gc_x, egl_x = gx[0:c], gx[c:2 * c], gx[2 * c:3 * c], gx[3 * c:3 * c + 1]
    gc_t = jnp.concatenate([gc, gc], axis=0).T
    kbeta = k * beta_x
    vbeta = v * beta_x
    wrhs = kbeta * egc_x
    qd = q * egc_x
    kd = k * ekd_x
    lane = lax.broadcasted_iota(jnp.int32, (1, LANE), 1)
    lo = lane < DN_DK
    pairs = []
    for p in range(DN_HEADS // 2):
        sl = slice(LANE * p, LANE * p + LANE)
        k_s, q_s = k[:, sl], q[:, sl]
        kb_s, vb_s, wr_s = kbeta[:, sl], vbeta[:, sl], wrhs[:, sl]
        gcx_s = gc_x[:, sl]
        k_sb = k_s.astype(BF16)
        k_sb2 = jnp.concatenate([k_sb, k_sb], axis=0)
        mats, rhss, attns = [], [], []
        for j in range(2):
            h = 2 * p + j
            mine = lo if j == 0 else jnp.logical_not(lo)
            gcx_r = pltpu.roll(gcx_s, DN_DK, 1)
            gcol = jnp.where(lo, gcx_s, gcx_r) if j == 0 else jnp.where(lo, gcx_r, gcx_s)
            grow = gc_t[DN_HEADS * d + h:DN_HEADS * d + h + 1, :]
            diff = gcol - grow
            dec = jnp.where(incl, jnp.exp(jnp.where(incl, diff, 0.0)), 0.0)
            kk = _dot_nt(jnp.where(mine, kb_s, 0.0).astype(BF16), k_sb2)
            mats.append(jnp.where(strict, kk * dec, 0.0))
            qk = _dot_nt(jnp.where(mine, q_s, 0.0).astype(BF16), k_sb)
            attns.append((qk * dec[:, :c]).astype(BF16))
            if j == 0:
                rhss.append(jnp.where(lo, vb_s, pltpu.roll(wr_s, DN_DK, 1)))
            else:
                rhss.append(jnp.where(lo, pltpu.roll(vb_s, DN_DK, 1), wr_s))
        pairs.append(dict(mats=mats, rhss=rhss, attns=attns, qd=qd[:, sl].astype(BF16),
                          kd_t=kd[:, sl].T.astype(BF16), egl=egl_x[:, sl]))
    return pairs


def _dn_solve(mats, rhss):
    c = DN_CHUNK
    lo = lax.broadcasted_iota(jnp.int32, (1, 2 * c), 1) < c
    pw, xs = list(mats), list(rhss)
    rounds = 6
    for r in range(rounds):
        for n in range(len(pw)):
            ph = pw[n].astype(BF16)
            b = jnp.concatenate([xs[n], pw[n]], axis=1) if r < rounds - 1 else xs[n]
            bh = b.astype(BF16)
            if r < DN_EXACT_ROUNDS:
                plo = (pw[n] - ph.astype(F32)).astype(BF16)
                lhs = jnp.concatenate([jnp.where(lo, ph, plo), ph[:, :c]], axis=1)
                bl = (b - bh.astype(F32)).astype(BF16)
                both = _dot(lhs, jnp.concatenate([bh, bh, bl], axis=0))
            else:
                both = _dot(ph[:, :c], bh)
            px = both[:, :2 * c]
            if r < rounds - 1:
                pw[n] = both[:, 2 * c:]
            xs[n] = xs[n] - px if r == 0 else xs[n] + px
    return xs


def _dn_scan_kernel(alog_ref, dtb_ref, eg_ref, eb_ref,
                    qf, kf, vf, abf, qb, kb, vb, abb, of_ref, ob_ref, s_ref):
    i = pl.program_id(1)

    @pl.when(i == 0)
    def _():
        s_ref[...] = jnp.zeros_like(s_ref)

    alog = alog_ref[...]
    dtb = dtb_ref[...]
    nch = TM // DN_CHUNK
    npair = DN_HEADS // 2
    lane = lax.broadcasted_iota(jnp.int32, (1, LANE), 1)
    lo = lane < DN_DK
    ri2 = lax.broadcasted_iota(jnp.int32, (LANE, LANE), 0)
    ci2 = lax.broadcasted_iota(jnp.int32, (LANE, LANE), 1)
    bdiag = (ri2 < DN_DK) == (ci2 < DN_DK)

    def body(it, carry):
        pairs, rows_of = [], []
        for g in range(DN_GROUP):
            cidx = it * DN_GROUP + g
            rf = pl.ds(pl.multiple_of(cidx * DN_CHUNK, DN_CHUNK), DN_CHUNK)
            rb = pl.ds(pl.multiple_of((nch - 1 - cidx) * DN_CHUNK, DN_CHUNK), DN_CHUNK)
            pairs += (_dn_pre(0, rf, qf, kf, vf, abf, alog, dtb, eg_ref, eb_ref)
                      + _dn_pre(1, rb, qb, kb, vb, abb, alog, dtb, eg_ref, eb_ref))
            rows_of += [rf] * npair + [rb] * npair
        xs = _dn_solve([m for pr in pairs for m in pr["mats"]], [r for pr in pairs for r in pr["rhss"]])
        for m, pr in enumerate(pairs):
            n = m % (2 * npair)
            d, p = divmod(n, npair)
            x0, x1 = xs[2 * m], xs[2 * m + 1]
            u = jnp.where(lo, x0, pltpu.roll(x1, DN_DK, 1))
            w = jnp.where(lo, pltpu.roll(x0, DN_DK, 1), x1)
            s = s_ref[n]
            sb = s.astype(BF16)
            v_new = u - _dot(w.astype(BF16), sb)
            vn_b = v_new.astype(BF16)
            o = _dot(pr["qd"], sb)
            o = o + _dot(pr["attns"][0], jnp.where(lo, vn_b, jnp.zeros_like(vn_b)))
            o = o + _dot(pr["attns"][1], jnp.where(lo, jnp.zeros_like(vn_b), vn_b))
            upd = _dot(pr["kd_t"], vn_b)
            s_ref[n] = s * pr["egl"] + jnp.where(bdiag, upd, 0.0)
            o_ref = of_ref if d == 0 else ob_ref
            o_ref[rows_of[m], LANE * p:LANE * p + LANE] = o
        return carry

    lax.fori_loop(0, nch // DN_GROUP, body, 0)


def _dn_scan(q, k, v, ab, alog, dtb, eg, eb):
    nb, ntot, w = q.shape
    nt = ntot // TM
    nl = nt - 1
    fwd = lambda b, i: (b, jnp.where(i == 0, nl, i - 1), 0)
    bwd = lambda b, i: (b, jnp.where(i == 0, nl, nl - i), 0)
    const = lambda a: pl.BlockSpec(a.shape, lambda b, i: (0,) * a.ndim)
    blk = lambda ww, im: pl.BlockSpec((None, TM, ww), im)
    return pl.pallas_call(
        _dn_scan_kernel,
        grid=(nb, nt),
        in_specs=[const(alog), const(dtb), const(eg), const(eb),
                  blk(w, fwd), blk(w, fwd), blk(w, fwd), blk(LANE, fwd),
                  blk(w, bwd), blk(w, bwd), blk(w, bwd), blk(LANE, bwd)],
        out_specs=[blk(w, fwd), blk(w, bwd)],
        out_shape=[jax.ShapeDtypeStruct((nb, ntot, w), F32)] * 2,
        scratch_shapes=[pltpu.VMEM((2 * (DN_HEADS // 2), LANE, LANE), F32)],
        compiler_params=_params(("parallel", "arbitrary")),
        name="dn_scan",
    )(alog, dtb, eg, eb, q, k, v, ab, q, k, v, ab)


def _diff_attn_kernel(lam_init, q_ref, k_ref, vt_ref, lam_ref, g_ref, ones_ref, o_ref, st_ref):
    i = pl.program_id(1)
    nl = pl.num_programs(1) - 1
    lv = lam_ref[...]
    lam = (jnp.exp(jnp.sum(lv[0:1] * lv[1:2], axis=-1, keepdims=True))
           - jnp.exp(jnp.sum(lv[2:3] * lv[3:4], axis=-1, keepdims=True)) + lam_init)
    lane = lax.broadcasted_iota(jnp.int32, (1, B_QK_W), 1)

    nsm = 2 * DA_HEADS

    def run(groups):
        q = q_ref[...]
        m_prev = None
        res = []
        for n in range(nsm + 1):
            if n < nsm:
                slot = slice(DA_QK * n // LANE * LANE, DA_QK * n // LANE * LANE + LANE)
                lo = DA_QK * n
                qm = jnp.where(jnp.logical_and(lane >= lo, lane < lo + DA_QK), q, jnp.zeros_like(q))[:, slot]
            hp = (n - 1) // 2
            m8 = jnp.full((8, TM), -jnp.inf, F32)
            acc = jnp.zeros((DA_V + 16, TM), F32)
            for grp in groups:
                rows = slice(grp[0] * TM, (grp[-1] + 1) * TM)
                nk = len(grp) * TM
                if n < nsm:
                    st = _dot_nt(k_ref[rows, slot], qm)
                    st_ref[n % 2, rows, :] = st
                    m8 = jnp.maximum(m8, jnp.max(st.reshape(nk // 8, 8, TM), axis=0))
                if n > 0:
                    e = jnp.exp2(st_ref[(n - 1) % 2, rows, :] - m_prev).astype(BF16)
                    vt = jnp.concatenate([vt_ref[c, DA_V * hp:DA_V * hp + DA_V, :] for c in grp], axis=1)
                    lhs = jnp.concatenate([vt, jnp.ones((16, nk), BF16)], axis=0)
                    acc = acc + _dot(lhs, e)
            if n > 0:
                res.append(acc[:DA_V] / acc[DA_V:DA_V + 1])
            if n < nsm:
                m_prev = jnp.max(m8, axis=0, keepdims=True)
        ot = jnp.concatenate([res[2 * h] - lam * res[2 * h + 1] for h in range(DA_HEADS)], axis=0)
        ms = _dot_exact_lhs(ones_ref[...], ot * ot, parts=2)
        yt = (ot * lax.rsqrt(ms + EPS)) * g_ref[...]
        o_ref[...] = (yt * (1.0 - lam_init)).T.astype(BF16)

    ntiles = st_ref.shape[1] // TM

    @pl.when(i < nl)
    def _():
        run([tuple(range(c, min(c + DA_KEY_GROUP, ntiles))) for c in range(0, ntiles, DA_KEY_GROUP)])

    @pl.when(i == nl)
    def _():
        run([(ntiles - 1,)])


def _diff_attn(q, k, vt, lam_vecs, subln_g, ones_b, lam_init):
    nb, ntot, w = q.shape
    nt = ntot // TM
    row = pl.BlockSpec((None, TM, w), lambda b, i: (b, i, 0))
    full = pl.BlockSpec((None, ntot, w), lambda b, i: (b, 0, 0))
    full_t = pl.BlockSpec((None,) + vt.shape[1:], lambda b, i: (b, 0, 0, 0))
    const = lambda a: pl.BlockSpec(a.shape, lambda b, i: (0,) * a.ndim)
    return pl.pallas_call(
        functools.partial(_diff_attn_kernel, lam_init),
        grid=(nb, nt),
        in_specs=[row, full, full_t, const(lam_vecs), const(subln_g), const(ones_b)],
        out_specs=row,
        out_shape=jax.ShapeDtypeStruct((nb, ntot, w), BF16),
        scratch_shapes=[pltpu.VMEM((2, ntot, TM), F32)],
        compiler_params=_params(("parallel", "arbitrary")),
        name="diff_attn",
    )(q, k, vt, lam_vecs, subln_g, ones_b)


def _win_attn_kernel(n_lat, sink_ref, q_ref, k_ref, v_ref, o_ref):
    i = pl.program_id(1)
    nl = pl.num_programs(1) - 1
    rep = WA_HEADS // WA_KV_HEADS
    lane = lax.broadcasted_iota(jnp.int32, (1, LANE), 1)
    lo = lane < WA_DIM
    rowg = lax.broadcasted_iota(jnp.int32, (rep * TM, 1), 0) // TM

    def run(k_all, v_all, bias):
        outs = []
        for g in range(WA_KV_HEADS):
            mine = lo if g == 0 else jnp.logical_not(lo)
            q3 = jnp.concatenate(
                [jnp.where(mine, q_ref[:, LANE * s:LANE * s + LANE], jnp.zeros((TM, LANE), BF16))
                 for s in range(rep)], axis=0)
            s = _dot_nt(q3, k_all)
            if bias is not None:
                s = s + bias
            sk = jnp.zeros((rep * TM, 1), F32)
            for r in range(rep):
                sk = jnp.where(rowg == r, sink_ref[rep * g + r], sk)
            m = jnp.maximum(jnp.max(s, axis=-1, keepdims=True), sk)
            e = jnp.exp(s - m)
            den = jnp.sum(e, axis=-1, keepdims=True) + jnp.exp(sk - m)
            outs.append(_dot(e.astype(BF16), v_all) / den)
        for s in range(rep):
            o_ref[:, LANE * s:LANE * s + LANE] = jnp.where(
                lo, outs[0][TM * s:TM * s + TM], outs[1][TM * s:TM * s + TM]).astype(BF16)

    kc = k_ref[n_lat:, :]
    vc = v_ref[n_lat:, :]
    band = TM + 2 * WINDOW

    @pl.when(i < nl)
    def _():
        start = pl.multiple_of(jnp.clip(i * TM - WINDOW, 0, n_lat - band), WINDOW)
        kb = k_ref[pl.ds(start, band), :]
        vb = v_ref[pl.ds(start, band), :]
        qpos = i * TM + lax.broadcasted_iota(jnp.int32, (TM, 1), 0)
        kpos = start + lax.broadcasted_iota(jnp.int32, (1, band), 1)
        near = jnp.where(jnp.abs(qpos - kpos) <= WINDOW, 0.0, -1e30)
        bias = jnp.concatenate([near, jnp.zeros((TM, kc.shape[0]), F32)], axis=1)
        bias = jnp.concatenate([bias] * rep, axis=0)
        run(jnp.concatenate([kb, kc], axis=0), jnp.concatenate([vb, vc], axis=0), bias)

    @pl.when(i == nl)
    def _():
        run(kc, vc, None)


def _win_attn(q, k, v, sink, n_lat):
    nb, ntot, w = q.shape
    nt = ntot // TM
    kw = k.shape[-1]
    row = pl.BlockSpec((None, TM, w), lambda b, i: (b, i, 0))
    full = pl.BlockSpec((None, ntot, kw), lambda b, i: (b, 0, 0))
    return pl.pallas_call(
        functools.partial(_win_attn_kernel, n_lat),
        grid=(nb, nt),
        in_specs=[pl.BlockSpec(memory_space=pltpu.SMEM), row, full, full],
        out_specs=row,
        out_shape=jax.ShapeDtypeStruct((nb, ntot, w), BF16),
        compiler_params=_params(("parallel", "arbitrary")),
        name="win_attn",
    )(sink, q, k, v)


def _out_proj_kernel(x_ref, of_ref, ob_ref, gate_ref, yb_ref, yw_ref, wa_ref, wb_ref, wc_ref, dng_ref,
                     ones_ref, g1_ref, sc2_ref, sh2_ref, n2_ref, wr_ref, br_ref, xn_o, h2_o, lg_o):
    o = of_ref[...] + ob_ref[...]
    ms = _dot_exact_rhs(o * o, ones_ref[...], parts=2)
    ya = ((o * lax.rsqrt(ms + EPS)) * dng_ref[...]) * _silu(gate_ref[...])
    y = _dot(ya.astype(BF16), wa_ref[...]) + _dot(yb_ref[...], wb_ref[...]) + _dot(yw_ref[...], wc_ref[...])
    xn = x_ref[...] + g1_ref[...] * y
    xn_o[...] = xn
    h2 = _rms_mod(xn, n2_ref[...], sc2_ref[...], sh2_ref[...])
    h2_o[...] = h2.astype(BF16)
    lg_o[...] = _dot3(h2, wr_ref[...]) + br_ref[...]


def _out_proj(x, of, ob, gate, yb, yw, wa, wb, wc, dng, ones_a, mod, n2g, wr, br, nb, ntot):
    d = x.shape[-1]
    nt = ntot // TM
    nl = nt - 1
    row = lambda w: pl.BlockSpec((None, TM, w), lambda b, i: (b, i, 0))
    const = lambda a: pl.BlockSpec(a.shape, lambda b, i: (0,) * a.ndim)
    return pl.pallas_call(
        _out_proj_kernel,
        grid=(nb, nt),
        in_specs=[row(d), row(A_W), row(A_W), row(A_W), row(B_W), row(C_W),
                  const(wa), const(wb), const(wc), const(dng), const(ones_a),
                  _mod_spec_d(2, nb, nl, d), _mod_spec_d(4, nb, nl, d), _mod_spec_d(3, nb, nl, d),
                  const(n2g), const(wr), const(br)],
        out_specs=[row(d), row(d), row(LANE)],
        out_shape=[jax.ShapeDtypeStruct((nb, ntot, d), F32), jax.ShapeDtypeStruct((nb, ntot, d), BF16),
                   jax.ShapeDtypeStruct((nb, ntot, LANE), F32)],
        compiler_params=_params(("parallel", "arbitrary")),
        name="out_proj",
    )(x, of, ob, gate, yb, yw, wa, wb, wc, dng, ones_a, mod, mod, mod, n2g, wr, br)


def _route_kernel(lg_ref, gw_o, pos_o, cnt_o):
    t = lg_ref.shape[0]
    lg = lg_ref[...]
    lane_i = lax.broadcasted_iota(jnp.int32, (1, LANE), 1)
    lane = lane_i.astype(F32)
    big = float(LANE)
    neg = -jnp.inf
    is_g = lane_i < N_GROUPS
    lgm = jnp.where(is_g, lg, neg)
    mg = jnp.max(lgm, axis=-1, keepdims=True)
    p_sel = 1.0 / jnp.sum(jnp.where(is_g, jnp.exp(lgm - mg), 0.0), axis=-1, keepdims=True)
    gidx = jnp.min(jnp.where(jnp.logical_and(is_g, lgm == mg), lane, big), axis=-1, keepdims=True)
    e_lane = lane_i - N_GROUPS
    in_grp = jnp.logical_and(jnp.logical_and(e_lane >= 0, e_lane < N_EXPERTS),
                             jnp.floor((lane - N_GROUPS) * (1.0 / EXP_PER_GROUP)) == gidx)
    le = jnp.where(in_grp, lg, neg)
    v1 = jnp.max(le, axis=-1, keepdims=True)
    i1 = jnp.min(jnp.where(jnp.logical_and(in_grp, le == v1), lane, big), axis=-1, keepdims=True)
    is1 = lane == i1
    le2 = jnp.where(is1, neg, le)
    v2 = jnp.max(le2, axis=-1, keepdims=True)
    rest = jnp.logical_and(in_grp, jnp.logical_not(is1))
    i2 = jnp.min(jnp.where(jnp.logical_and(rest, le2 == v2), lane, big), axis=-1, keepdims=True)
    is2 = lane == i2
    e2 = jnp.exp(v2 - v1)
    w1 = 1.0 / (1.0 + e2)
    w2 = e2 / (1.0 + e2)
    gw = jnp.where(is1, p_sel * w1, jnp.where(is2, p_sel * w2, 0.0))
    sel = jnp.logical_or(jnp.logical_or(is1, is2), lane == gidx)
    self_ = jnp.where(sel, 1.0, 0.0)
    ri = lax.broadcasted_iota(jnp.int32, (ROUTE_BLK, ROUTE_BLK), 0)
    ci = lax.broadcasted_iota(jnp.int32, (ROUTE_BLK, ROUTE_BLK), 1)
    tri = jnp.where(ri > ci, 1.0, 0.0).astype(BF16)
    run = jnp.zeros((1, LANE), F32)
    pos_blocks = []
    for b in range(t // ROUTE_BLK):
        blk = self_[b * ROUTE_BLK:(b + 1) * ROUTE_BLK]
        pos_blocks.append(_dot(tri, blk.astype(BF16)) + run)
        run = run + jnp.sum(blk, axis=0, keepdims=True)
    pos = jnp.where(sel, jnp.concatenate(pos_blocks, axis=0), -1.0)
    gw_o[...] = gw.T
    pos_o[...] = pos.T
    cnt_o[...] = jnp.broadcast_to(run, (8, LANE)).astype(jnp.int32)


def _route(logits):
    ntok = logits.shape[0]
    ntile = ntok // MOE_T
    return pl.pallas_call(
        _route_kernel,
        grid=(ntile,),
        in_specs=[pl.BlockSpec((MOE_T, LANE), lambda t: (t, 0))],
        out_specs=[pl.BlockSpec((None, LANE, MOE_T), lambda t: (t, 0, 0)),
                   pl.BlockSpec((None, LANE, MOE_T), lambda t: (t, 0, 0)),
                   pl.BlockSpec((None, 8, LANE), lambda t: (t, 0, 0))],
        out_shape=[jax.ShapeDtypeStruct((ntile, LANE, MOE_T), F32),
                   jax.ShapeDtypeStruct((ntile, LANE, MOE_T), F32),
                   jax.ShapeDtypeStruct((ntile, 8, LANE), jnp.int32)],
        compiler_params=_params(("parallel",)),
        name="moe_route",
    )(logits)


def _moe_kernel(cnt_ref, h_ref, pos_ref, gw_ref, wg_ref, wu_ref, wd_ref, o_ref):
    t = pl.program_id(0)
    g = pl.program_id(1)

    @pl.when(g == 0)
    def _():
        o_ref[...] = jnp.zeros_like(o_ref)

    n = cnt_ref[t * N_GROUPS + g]
    prow = pos_ref[pl.ds(g, 1), :]
    grows = [gw_ref[pl.ds(N_GROUPS + EXP_PER_GROUP * g + e, 1), :] for e in range(EXP_PER_GROUP)]

    def chunk(c, carry):
        slot = (lax.broadcasted_iota(jnp.int32, (MOE_CH, 1), 0) + c * MOE_CH).astype(F32)
        hit = prow == slot
        onehot = jnp.where(hit, 1.0, 0.0).astype(BF16)
        hc = _dot(onehot, h_ref[...]).astype(BF16)
        y = jnp.zeros((MOE_CH, o_ref.shape[1]), F32)
        for e in range(EXP_PER_GROUP):
            gcol = jnp.sum(jnp.where(hit, grows[e], 0.0), axis=-1, keepdims=True)
            hid = _silu(_dot(hc, wg_ref[e])) * _dot(hc, wu_ref[e])
            y = y + _dot((hid * gcol).astype(BF16), wd_ref[e])
        o_ref[...] += _dot_tn(onehot, y.astype(BF16))
        return carry

    lax.fori_loop(0, (n + MOE_CH - 1) // MOE_CH, chunk, 0)


def _moe(counts, h2, pos_t, gw_t, wg, wu, wd):
    ntok, d = h2.shape
    ntile = ntok // MOE_T
    gs = pltpu.PrefetchScalarGridSpec(
        num_scalar_prefetch=1,
        grid=(ntile, N_GROUPS),
        in_specs=[pl.BlockSpec((MOE_T, d), lambda t, g, c: (t, 0)),
                  pl.BlockSpec((None, LANE, MOE_T), lambda t, g, c: (t, 0, 0)),
                  pl.BlockSpec((None, LANE, MOE_T), lambda t, g, c: (t, 0, 0)),
                  pl.BlockSpec((EXP_PER_GROUP, d, D_EXPERT), lambda t, g, c: (g, 0, 0)),
                  pl.BlockSpec((EXP_PER_GROUP, d, D_EXPERT), lambda t, g, c: (g, 0, 0)),
                  pl.BlockSpec((EXP_PER_GROUP, D_EXPERT, d), lambda t, g, c: (g, 0, 0))],
        out_specs=pl.BlockSpec((MOE_T, d), lambda t, g, c: (t, 0)),
    )
    return pl.pallas_call(
        _moe_kernel,
        grid_spec=gs,
        out_shape=jax.ShapeDtypeStruct((ntok, d), F32),
        compiler_params=_params(("parallel", "arbitrary")),
        name="moe_experts",
    )(counts, h2, pos_t, gw_t, wg, wu, wd)


def _final_kernel(xn_ref, ff_ref, g2_ref, g_ref, o_ref):
    x = xn_ref[...] + g2_ref[...] * ff_ref[...]
    y = x * lax.rsqrt(jnp.mean(x * x, axis=-1, keepdims=True) + EPS)
    o_ref[...] = y * g_ref[...]


def _final(xn, ff, mod, g, nb, n_lat, ntot):
    d = xn.shape[-1]
    nl = n_lat // TM
    row = pl.BlockSpec((None, TM, d), lambda b, i: (b, i, 0))
    return pl.pallas_call(
        _final_kernel,
        grid=(nb, nl),
        in_specs=[row, row, _mod_spec_d(5, nb, nl, d), pl.BlockSpec(g.shape, lambda b, i: (0, 0))],
        out_specs=row,
        out_shape=jax.ShapeDtypeStruct((nb, n_lat, d), F32),
        compiler_params=_params(("parallel", "arbitrary")),
        name="final_norm",
    )(xn, ff, mod, g)


def _rope_swap_perm(width, dim):
    nf = dim // 4
    j = jnp.arange(width)
    base = (j // (2 * nf)) * (2 * nf)
    return base + (j % (2 * nf) + nf) % (2 * nf)


def _rope_tables(n_lat, n_ctx, dim, width):
    nf = dim // 4
    t = jnp.arange(n_lat)
    row = (t // GRID_W).astype(F32)
    col = (t % GRID_W).astype(F32)
    inv = ROPE_BASE ** (-jnp.arange(nf, dtype=F32) / nf)
    j = jnp.arange(width) % dim
    axis = j // (2 * nf)
    pos = jnp.where(axis[None, :] == 0, row[:, None], col[:, None])
    ang = pos * inv[j % nf][None, :]
    sign = jnp.where(j % (2 * nf) < nf, -1.0, 1.0).astype(F32)
    cos = jnp.concatenate([jnp.cos(ang), jnp.ones((n_ctx, width), F32)], axis=0)
    sin = jnp.concatenate([jnp.sin(ang) * sign[None, :], jnp.zeros((n_ctx, width), F32)], axis=0)
    return cos, sin


def _block_ones(width, group, value):
    j = jnp.arange(width)
    return jnp.where((j[:, None] // group) == (j[None, :] // group), value, 0.0).astype(BF16)


def _slot_cols():
    rep = WA_HEADS // WA_KV_HEADS
    cols = []
    for s in range(rep):
        cols.append(jnp.arange(WA_DIM) + WA_DIM * s)
        cols.append(jnp.arange(WA_DIM) + WA_DIM * (rep + s))
    return jnp.concatenate(cols)


def _build_w_in(w):
    d = w.shape[0]
    s1, s2 = A_COLS, A_COLS + B_COLS
    wa, wb, wc = w[:, :s1], w[:, s1:s2], w[:, s2:]
    ab = jnp.pad(wa[:, QKV_W + A_W:], ((0, 0), (0, LANE - 4 * DN_HEADS)))
    bq, bk, bv = wb[:, :B_QK_W], wb[:, B_QK_W:2 * B_QK_W], wb[:, 2 * B_QK_W:]
    pb = _rope_swap_perm(B_QK_W, DA_QK)
    cq = wc[:, :C_W][:, _slot_cols()]
    ck, cv = wc[:, C_W:C_W + C_KV_W], wc[:, C_W + C_KV_W:]
    pcq = _rope_swap_perm(C_W, WA_DIM)
    pck = _rope_swap_perm(C_KV_W, WA_DIM)
    cat = jnp.concatenate([wa[:, :QKV_W], wa[:, QKV_W:QKV_W + A_W], ab,
                           bq, bk, bv, bq[:, pb], bk[:, pb],
                           cq, ck, cv, cq[:, pcq], ck[:, pck]], axis=1)
    assert cat.shape == (d, _O_END)
    return cat.astype(BF16)


def _expand_mats():
    r = jnp.arange(LANE)[:, None]
    h = (jnp.arange(A_QK_W) // DN_DK)[None, :]
    eg = jnp.stack([(r == DN_HEADS * d + h) for d in range(2)]).astype(BF16)
    eb = jnp.stack([(r == 2 * DN_HEADS + DN_HEADS * d + h) for d in range(2)]).astype(BF16)
    return eg, eb


def kernel(x, c, ctx, c_ctx, ada_w, ada_b, norm1_g, norm2_g, w_in, dn_conv_w, dn_a_log, dn_dt_bias, dn_norm_g, da_lambda, da_subln_g, wa_sink, w_out, router_group_w, router_group_b, router_expert_w, router_expert_b, exp_w_gate, exp_w_up, exp_w_down, final_norm_g):
    nb, n_lat, d = x.shape
    n_ctx = ctx.shape[1]
    depth = ada_w.shape[0]
    assert n_ctx == TM and n_lat % TM == 0 and n_lat >= 3 * TM
    ntot = n_lat + n_ctx
    ntok = nb * ntot
    assert ntok % MOE_T == 0

    xa = jnp.concatenate([x, ctx], axis=1)
    cc = jnp.zeros((16, d), F32).at[:nb].set(c).at[nb].set(c_ctx)
    cosb, sinb = _rope_tables(n_lat, n_ctx, DA_QK, B_QK_W)
    cosc, sinc = _rope_tables(n_lat, n_ctx, WA_DIM, C_KV_W)
    tabs = (cosb, sinb, cosc, sinc)
    ones_a = _block_ones(A_W, DN_DV, 1.0)
    mean_a = _block_ones(A_W, DN_DV, 1.0 / DN_DV)
    mean_b = _block_ones(B_W, DA_V, 1.0 / DA_V)
    eg, eb = _expand_mats()
    slot_rows = _slot_cols()
    pad_lane = lambda v: jnp.pad(v.reshape(1, -1), ((0, 0), (0, LANE - v.size)))

    mods = [_ada(cc, ada_w[li], ada_b[li]).reshape(16, 6, 1, d) for li in range(depth)]
    xn = ff = None
    for li in range(depth):
        mod = mods[li]
        w_cat = _build_w_in(w_in[li])
        n1g = norm1_g[li].reshape(1, d)
        if li == 0:
            outs = _in_proj((xa,), (mod,), n1g, w_cat, tabs, nb, ntot)
            xcur = xa
        else:
            outs = _in_proj((xn, ff.reshape(nb, ntot, d)), (mod, mods[li - 1]), n1g, w_cat, tabs, nb, ntot)
            xcur, outs = outs[0], outs[1:]
        zqkv, gate, ab, qb, kb, vb, qc, kc, vc = outs

        conv_w8 = jnp.pad(dn_conv_w[li], ((0, 8 - DN_CONV), (0, 0)))
        q, k, v = _dn_prep(zqkv, conv_w8, ones_a)
        of, ob = _dn_scan(q, k, v, ab, pad_lane(dn_a_log[li]), pad_lane(dn_dt_bias[li]), eg, eb)

        lam_init = 0.8 - 0.6 * math.exp(-0.3 * li)
        yb = _diff_attn(qb, kb, vb, da_lambda[li], jnp.tile(da_subln_g[li], DA_HEADS).reshape(B_W, 1),
                        mean_b, lam_init)
        yw = _win_attn(qc, kc, vc, jnp.pad(wa_sink[li], (0, 8 - WA_HEADS)), n_lat)

        wo = w_out[li]
        wa_o = wo[:A_W].astype(BF16)
        wb_o = wo[A_W:A_W + B_W].astype(BF16)
        wc_o = wo[A_W + B_W:][slot_rows].astype(BF16)
        wr = jnp.pad(jnp.concatenate([router_group_w[li], router_expert_w[li]], axis=1),
                     ((0, 0), (0, LANE - N_GROUPS - N_EXPERTS)))
        br = pad_lane(jnp.concatenate([router_group_b[li], router_expert_b[li]]))
        xn, h2, logits = _out_proj(xcur, of, ob, gate, yb, yw, wa_o, wb_o, wc_o,
                                   jnp.tile(dn_norm_g[li], DN_HEADS).reshape(1, A_W), mean_a, mod,
                                   norm2_g[li].reshape(1, d), wr, br, nb, ntot)

        gw_t, pos_t, cnt = _route(logits.reshape(ntok, LANE))
        counts = cnt[:, 0, :N_GROUPS].reshape(-1)
        ff = _moe(counts, h2.reshape(ntok, d), pos_t, gw_t, exp_w_gate[li].astype(BF16),
                  exp_w_up[li].astype(BF16), exp_w_down[li].astype(BF16))

    return _final(xn, ff.reshape(nb, ntot, d), mods[depth - 1], final_norm_g.reshape(1, d), nb, n_lat, ntot)
```

```python
import functools
import math

import jax
import jax.numpy as jnp
from jax import lax
from jax.experimental import pallas as pl
from jax.experimental.pallas import tpu as pltpu

F32 = jnp.float32
BF16 = jnp.bfloat16

GRID_W = 64
EPS = 1e-6
LOG2E = math.log2(math.e)
ROPE_BASE = 10000.0
DN_HEADS = 6
DN_DK = 64
DN_DV = 64
DN_CONV = 5
DN_CHUNK = 64
DA_HEADS = 4
DA_QK = 32
DA_V = 64
WA_HEADS = 6
WA_KV_HEADS = 2
WA_DIM = 64
WINDOW = 128
N_GROUPS = 4
EXP_PER_GROUP = 4
N_EXPERTS = 16
D_EXPERT = 512

A_QK_W = DN_HEADS * DN_DK
A_W = DN_HEADS * DN_DV
QKV_W = 2 * A_QK_W + A_W
B_W = DA_HEADS * DA_V
B_QK_W = 2 * DA_HEADS * DA_QK
C_W = WA_HEADS * WA_DIM
C_KV_W = WA_KV_HEADS * WA_DIM
A_COLS = QKV_W + A_W + 4 * DN_HEADS
B_COLS = 2 * B_QK_W + B_W

LANE = 128
TM = 256
MOE_T = 1024
MOE_CH = 288
ROUTE_BLK = 256
DN_EXACT_ROUNDS = 5
DA_KEY_GROUP = 4
VMEM_LIMIT = 56 * 1024 * 1024

_O_QKV = 0
_O_GATE = _O_QKV + QKV_W
_O_AB = _O_GATE + A_W
_O_BQ = _O_AB + LANE
_O_BK = _O_BQ + B_QK_W
_O_BV = _O_BK + B_QK_W
_O_BQS = _O_BV + B_W
_O_BKS = _O_BQS + B_QK_W
_O_CQ = _O_BKS + B_QK_W
_O_CK = _O_CQ + C_W
_O_CV = _O_CK + C_KV_W
_O_CQS = _O_CV + C_KV_W
_O_CKS = _O_CQS + C_W
_O_END = _O_CKS + C_KV_W


def _dot(a, b):
    return jnp.dot(a, b, preferred_element_type=F32)


def _dot_nt(a, b):
    return lax.dot_general(a, b, (((1,), (1,)), ((), ())), preferred_element_type=F32)


def _dot_tn(a, b):
    return lax.dot_general(a, b, (((0,), (0,)), ((), ())), preferred_element_type=F32)


def _split2(a):
    hi = a.astype(BF16)
    lo = (a - hi.astype(F32)).astype(BF16)
    return hi, lo


def _split3(a):
    hi = a.astype(BF16)
    r = a - hi.astype(F32)
    mid = r.astype(BF16)
    lo = (r - mid.astype(F32)).astype(BF16)
    return hi, mid, lo


def _dot3(a, b):
    ah, al = _split2(a)
    bh, bl = _split2(b)
    return _dot(ah, bh) + (_dot(ah, bl) + _dot(al, bh))


def _dot_exact_rhs(a, b_bf16, parts=3):
    sp = _split3(a) if parts == 3 else _split2(a)
    out = _dot(sp[0], b_bf16)
    for p in sp[1:]:
        out = out + _dot(p, b_bf16)
    return out


def _dot_exact_lhs(a_bf16, b, parts=3):
    sp = _split3(b) if parts == 3 else _split2(b)
    out = _dot(a_bf16, sp[0])
    for p in sp[1:]:
        out = out + _dot(a_bf16, p)
    return out


def _col_reduce(x, op, slab=64):
    n, w = x.shape
    if n > slab and n % slab == 0:
        x = op(x.reshape(n // slab, slab, w), axis=0)
    return op(x, axis=0, keepdims=True)


def _silu(x):
    return x * jax.nn.sigmoid(x)


def _softplus(x):
    return jnp.maximum(x, 0.0) + jnp.log1p(jnp.exp(-jnp.abs(x)))


def _params(sem):
    return pltpu.CompilerParams(dimension_semantics=sem, vmem_limit_bytes=VMEM_LIMIT)


def _ada_kernel(c_ref, w_ref, b_ref, o_ref):
    o_ref[...] = _dot3(_silu(c_ref[...]), w_ref[...]) + b_ref[...]


def _ada(cc, w, b):
    rows, d = cc.shape
    n = w.shape[1]
    tn = n // 4
    return pl.pallas_call(
        _ada_kernel,
        grid=(n // tn,),
        in_specs=[pl.BlockSpec((rows, d), lambda j: (0, 0)),
                  pl.BlockSpec((d, tn), lambda j: (0, j)),
                  pl.BlockSpec((1, tn), lambda j: (0, j))],
        out_specs=pl.BlockSpec((rows, tn), lambda j: (0, j)),
        out_shape=jax.ShapeDtypeStruct((rows, n), F32),
        compiler_params=_params(("arbitrary",)),
        name="ada_mod",
    )(cc, w, b.reshape(1, n))


def _mod_spec_d(k, nb, nl, d):
    return pl.BlockSpec((None, None, 1, d), lambda b, i: (jnp.where(i == nl, nb, b), k, 0, 0))


def _rms_mod(x, g, sc, sh):
    y = x * lax.rsqrt(jnp.mean(x * x, axis=-1, keepdims=True) + EPS)
    return (y * g) * (1.0 + sc) + sh


def _in_proj_kernel(fuse_res, *refs):
    if fuse_res:
        xn_ref, ff_ref, g2_ref = refs[:3]
        refs = refs[3:]
    else:
        xl_ref, xc_ref = refs[:2]
        refs = refs[2:]
    (sc_ref, sh_ref, g_ref, w_ref, cosb_ref, sinb_ref, cosc_ref, sinc_ref) = refs[:8]
    outs = refs[8:]
    if fuse_res:
        x = xn_ref[...] + g2_ref[...] * ff_ref[...]
    else:
        is_ctx = pl.program_id(1) == pl.num_programs(1) - 1
        x = jnp.where(is_ctx, xc_ref[...], xl_ref[...])
    outs[0][...] = x
    outs = outs[1:]
    (zqkv_o, gate_o, ab_o, qb_o, kb_o, vb_o, qc_o, kc_o, vc_o) = outs
    hb = _rms_mod(x, g_ref[...], sc_ref[...], sh_ref[...]).astype(BF16)

    def seg(a, b):
        return _dot(hb, w_ref[:, a:b])

    zqkv_o[...] = seg(_O_QKV, _O_GATE)
    gate_o[...] = seg(_O_GATE, _O_AB)
    ab_o[...] = seg(_O_AB, _O_BQ)
    cb = cosb_ref[...]
    sb = sinb_ref[...]
    qb_o[...] = ((seg(_O_BQ, _O_BK) * cb + seg(_O_BQS, _O_BKS) * sb) * (DA_QK ** -0.5 * LOG2E)).astype(BF16)
    kb_o[...] = (seg(_O_BK, _O_BV) * cb + seg(_O_BKS, _O_CQ) * sb).astype(BF16)
    vb_o[...] = seg(_O_BV, _O_BQS).T.astype(BF16)
    cc = cosc_ref[...]
    sc_ = sinc_ref[...]
    cc3 = jnp.concatenate([cc, cc, cc], axis=1)
    sc3 = jnp.concatenate([sc_, sc_, sc_], axis=1)
    qc_o[...] = ((seg(_O_CQ, _O_CK) * cc3 + seg(_O_CQS, _O_CKS) * sc3) * (WA_DIM ** -0.5)).astype(BF16)
    kc_o[...] = (seg(_O_CK, _O_CV) * cc + seg(_O_CKS, _O_END) * sc_).astype(BF16)
    vc_o[...] = seg(_O_CV, _O_CQS).astype(BF16)


def _in_proj(x_parts, mod, norm_g, w_cat, tabs, nb, ntot):
    fuse_res = x_parts[0] == "res"
    x_parts = x_parts[1:]
    d = x_parts[0].shape[-1]
    nt = ntot // TM
    nl = nt - 1
    row = lambda w: pl.BlockSpec((None, TM, w), lambda b, i: (b, i, 0))
    tab = lambda w: pl.BlockSpec((TM, w), lambda b, i: (i, 0))
    const = lambda a: pl.BlockSpec(a.shape, lambda b, i: (0,) * a.ndim)
    if fuse_res:
        (xn, ff), prev_mod = x_parts, mod[1]
        ins = [xn, ff, prev_mod]
        in_specs = [row(d), row(d), _mod_spec_d(5, nb, nl, d)]
        cur_mod = mod[0]
    else:
        ins = list(x_parts)
        in_specs = [pl.BlockSpec((None, TM, d), lambda b, i: (b, jnp.minimum(i, nl - 1), 0)),
                    pl.BlockSpec((None, TM, d), lambda b, i: (b, 0, 0))]
        cur_mod = mod[0]
    ins += [cur_mod, cur_mod, norm_g, w_cat, *tabs]
    in_specs += [_mod_spec_d(1, nb, nl, d), _mod_spec_d(0, nb, nl, d), const(norm_g), const(w_cat),
                 tab(B_QK_W), tab(B_QK_W), tab(C_KV_W), tab(C_KV_W)]
    widths = [(QKV_W, F32), (A_W, F32), (LANE, F32), (B_QK_W, BF16), (B_QK_W, BF16), (None, BF16),
              (C_W, BF16), (C_KV_W, BF16), (C_KV_W, BF16)]
    widths = [(d, F32)] + widths
    return pl.pallas_call(
        functools.partial(_in_proj_kernel, fuse_res),
        grid=(nb, nt),
        in_specs=in_specs,
        out_specs=[pl.BlockSpec((None, None, B_W, TM), lambda b, i: (b, i, 0, 0)) if w is None else row(w)
                   for w, _ in widths],
        out_shape=[jax.ShapeDtypeStruct((nb, nt, B_W, TM) if w is None else (nb, ntot, w), dt)
                   for w, dt in widths],
        compiler_params=_params(("parallel", "arbitrary")),
        name="in_proj",
    )(*ins)


def _dn_prep_kernel(zc_ref, zp_ref, zn_ref, w_ref, ones_ref, q_o, k_o, v_o, ext_ref):
    i = pl.program_id(1)
    nl = pl.num_programs(1) - 1
    prev_ok = jnp.logical_and(i >= 1, i <= nl - 1)
    next_ok = i <= nl - 2
    ext_ref[0:8, :] = jnp.where(prev_ok, zp_ref[...], 0.0)
    ext_ref[8:8 + TM, :] = zc_ref[...]
    ext_ref[8 + TM:16 + TM, :] = jnp.where(next_ok, zn_ref[...], 0.0)
    half = DN_CONV // 2
    acc = w_ref[0:1, :] * ext_ref[8 - half:8 - half + TM, :]
    for j in range(1, DN_CONV):
        acc = acc + w_ref[j:j + 1, :] * ext_ref[8 - half + j:8 - half + j + TM, :]
    y = _silu(acc)
    ones = ones_ref[...]

    def l2n(t):
        ss = _dot_exact_rhs(t * t, ones, parts=2)
        return t * lax.rsqrt(ss + EPS)

    q_o[...] = l2n(y[:, :A_QK_W]) * (DN_DK ** -0.5)
    k_o[...] = l2n(y[:, A_QK_W:2 * A_QK_W])
    v_o[...] = y[:, 2 * A_QK_W:]


def _dn_prep(zqkv, conv_w8, ones_a):
    nb, ntot, w = zqkv.shape
    nt = ntot // TM
    r8 = TM // 8
    row = lambda ww: pl.BlockSpec((None, TM, ww), lambda b, i: (b, i, 0))
    return pl.pallas_call(
        _dn_prep_kernel,
        grid=(nb, nt),
        in_specs=[row(w),
                  pl.BlockSpec((None, 8, w), lambda b, i: (b, jnp.maximum(i * r8 - 1, 0), 0)),
                  pl.BlockSpec((None, 8, w), lambda b, i: (b, jnp.minimum(i * r8 + r8, ntot // 8 - 1), 0)),
                  pl.BlockSpec(conv_w8.shape, lambda b, i: (0, 0)),
                  pl.BlockSpec(ones_a.shape, lambda b, i: (0, 0))],
        out_specs=[row(A_QK_W), row(A_QK_W), row(A_W)],
        out_shape=[jax.ShapeDtypeStruct((nb, ntot, A_QK_W), F32)] * 3,
        scratch_shapes=[pltpu.VMEM((TM + 16, w), F32)],
        compiler_params=_params(("parallel", "arbitrary")),
        name="dn_prep",
    )(zqkv, zqkv, zqkv, conv_w8, ones_a)


def _dn_pre(d, rows, q_ref, k_ref, v_ref, ab_ref, alog, dtb, eg_ref, eb_ref):
    c = DN_CHUNK
    q = q_ref[rows, :]
    k = k_ref[rows, :]
    v = v_ref[rows, :]
    ab = ab_ref[rows, :]
    g = -jnp.exp(alog) * _softplus(ab + dtb)
    beta = jax.nn.sigmoid(ab)
    ri = lax.broadcasted_iota(jnp.int32, (c, 2 * c), 0)
    ci = lax.broadcasted_iota(jnp.int32, (c, 2 * c), 1) % c
    if d == 0:
        incl, strict = ri >= ci, ri > ci
    else:
        incl, strict = ri <= ci, ri < ci
    cum = jnp.where(incl[:, :c], 1.0, 0.0).astype(BF16)
    gc = _dot_exact_lhs(cum, g)
    last = c - 1 if d == 0 else 0
    g_last = gc[last:last + 1, :]
    egc = jnp.exp(gc)
    ekd = jnp.exp(g_last - gc)
    egl = jnp.broadcast_to(jnp.exp(g_last), (8, LANE))
    eg = eg_ref[d]
    eb = eb_ref[d]
    beta_x = _dot_exact_rhs(beta, eb)
    gx = _dot_exact_rhs(jnp.concatenate([egc, ekd, gc, egl], axis=0), eg)
    egc_x, ekd_x, gc_x, egl_x = gx[0:c], gx[c:2 * c], gx[2 * c:3 * c], gx[3 * c:3 * c + 1]
    gc_t = jnp.concatenate([gc, gc], axis=0).T
    kbeta = k * beta_x
    vbeta = v * beta_x
    wrhs = kbeta * egc_x
    qd = q * egc_x
    kd = k * ekd_x
    lane = lax.broadcasted_iota(jnp.int32, (1, LANE), 1)
    lo = lane < DN_DK
    pairs = []
    for p in range(DN_HEADS // 2):
        sl = slice(LANE * p, LANE * p + LANE)
        k_s, q_s = k[:, sl], q[:, sl]
        kb_s, vb_s, wr_s = kbeta[:, sl], vbeta[:, sl], wrhs[:, sl]
        gcx_s = gc_x[:, sl]
        k_sb = k_s.astype(BF16)
        k_sb2 = jnp.concatenate([k_sb, k_sb], axis=0)
        mats, rhss, attns = [], [], []
        for j in range(2):
            h = 2 * p + j
            mine = lo if j == 0 else jnp.logical_not(lo)
            gcx_r = pltpu.roll(gcx_s, DN_DK, 1)
            gcol = jnp.where(lo, gcx_s, gcx_r) if j == 0 else jnp.where(lo, gcx_r, gcx_s)
            grow = gc_t[DN_HEADS * d + h:DN_HEADS * d + h + 1, :]
            diff = gcol - grow
            dec = jnp.where(incl, jnp.exp(jnp.where(incl, diff, 0.0)), 0.0)
            kk = _dot_nt(jnp.where(mine, kb_s, 0.0).astype(BF16), k_sb2)
            mats.append(jnp.where(strict, kk * dec, 0.0))
            qk = _dot_nt(jnp.where(mine, q_s, 0.0).astype(BF16), k_sb)
            attns.append((qk * dec[:, :c]).astype(BF16))
            if j == 0:
                rhss.append(jnp.where(lo, vb_s, pltpu.roll(wr_s, DN_DK, 1)))
            else:
                rhss.append(jnp.where(lo, pltpu.roll(vb_s, DN_DK, 1), wr_s))
        pairs.append(dict(mats=mats, rhss=rhss, attns=attns, qd=qd[:, sl].astype(BF16),
                          kd_t=kd[:, sl].T.astype(BF16), egl=egl_x[:, sl]))
    return pairs


def _dn_solve(mats, rhss):
    c = DN_CHUNK
    lo = lax.broadcasted_iota(jnp.int32, (1, 2 * c), 1) < c
    pw, xs = list(mats), list(rhss)
    rounds = 6
    for r in range(rounds):
        for n in range(len(pw)):
            last = r == rounds - 1
            ph = pw[n].astype(BF16)
            xh = xs[n].astype(BF16)
            bh = xh if last else jnp.concatenate([xh, ph], axis=1)
            if r < DN_EXACT_ROUNDS:
                plo = (pw[n] - ph.astype(F32)).astype(BF16)
                xl = (xs[n] - xh.astype(F32)).astype(BF16)
                bl = xl if last else jnp.concatenate([xl, plo], axis=1)
                lhs = jnp.concatenate([jnp.where(lo, ph, plo), ph[:, :c]], axis=1)
                both = _dot(lhs, jnp.concatenate([bh, bh, bl], axis=0))
            else:
                both = _dot(ph[:, :c], bh)
            px = both[:, :2 * c]
            if r < rounds - 1:
                pw[n] = both[:, 2 * c:]
            xs[n] = xs[n] - px if r == 0 else xs[n] + px
    return xs


def _dn_scan_kernel(alog_ref, dtb_ref, eg_ref, eb_ref,
                    qf, kf, vf, abf, qb, kb, vb, abb, of_ref, ob_ref, s_ref):
    i = pl.program_id(1)

    @pl.when(i == 0)
    def _():
        s_ref[...] = jnp.zeros_like(s_ref)

    alog = alog_ref[...]
    dtb = dtb_ref[...]
    nch = TM // DN_CHUNK
    npair = DN_HEADS // 2
    lane = lax.broadcasted_iota(jnp.int32, (1, LANE), 1)
    lo = lane < DN_DK
    ri2 = lax.broadcasted_iota(jnp.int32, (LANE, LANE), 0)
    ci2 = lax.broadcasted_iota(jnp.int32, (LANE, LANE), 1)
    bdiag = (ri2 < DN_DK) == (ci2 < DN_DK)

    pairs, rows_of = [], []
    for g in range(nch):
        rf = slice(g * DN_CHUNK, (g + 1) * DN_CHUNK)
        rb = slice((nch - 1 - g) * DN_CHUNK, (nch - g) * DN_CHUNK)
        pairs += (_dn_pre(0, rf, qf, kf, vf, abf, alog, dtb, eg_ref, eb_ref)
                  + _dn_pre(1, rb, qb, kb, vb, abb, alog, dtb, eg_ref, eb_ref))
        rows_of += [rf] * npair + [rb] * npair
    xs = _dn_solve([m for pr in pairs for m in pr["mats"]], [r for pr in pairs for r in pr["rhss"]])
    for m, pr in enumerate(pairs):
        n = m % (2 * npair)
        d, p = divmod(n, npair)
        x0, x1 = xs[2 * m], xs[2 * m + 1]
        u = jnp.where(lo, x0, pltpu.roll(x1, DN_DK, 1))
        w = jnp.where(lo, pltpu.roll(x0, DN_DK, 1), x1)
        s = s_ref[n]
        sb = s.astype(BF16)
        v_new = u - _dot(w.astype(BF16), sb)
        vn_b = v_new.astype(BF16)
        o = _dot(pr["qd"], sb)
        o = o + _dot(pr["attns"][0], jnp.where(lo, vn_b, jnp.zeros_like(vn_b)))
        o = o + _dot(pr["attns"][1], jnp.where(lo, jnp.zeros_like(vn_b), vn_b))
        upd = _dot(pr["kd_t"], vn_b)
        s_ref[n] = s * pr["egl"] + jnp.where(bdiag, upd, 0.0)
        o_ref = of_ref if d == 0 else ob_ref
        o_ref[rows_of[m], LANE * p:LANE * p + LANE] = o


def _dn_scan(q, k, v, ab, alog, dtb, eg, eb):
    nb, ntot, w = q.shape
    nt = ntot // TM
    nl = nt - 1
    fwd = lambda b, i: (b, jnp.where(i == 0, nl, i - 1), 0)
    bwd = lambda b, i: (b, jnp.where(i == 0, nl, nl - i), 0)
    const = lambda a: pl.BlockSpec(a.shape, lambda b, i: (0,) * a.ndim)
    blk = lambda ww, im: pl.BlockSpec((None, TM, ww), im)
    return pl.pallas_call(
        _dn_scan_kernel,
        grid=(nb, nt),
        in_specs=[const(alog), const(dtb), const(eg), const(eb),
                  blk(w, fwd), blk(w, fwd), blk(w, fwd), blk(LANE, fwd),
                  blk(w, bwd), blk(w, bwd), blk(w, bwd), blk(LANE, bwd)],
        out_specs=[blk(w, fwd), blk(w, bwd)],
        out_shape=[jax.ShapeDtypeStruct((nb, ntot, w), F32)] * 2,
        scratch_shapes=[pltpu.VMEM((2 * (DN_HEADS // 2), LANE, LANE), F32)],
        compiler_params=_params(("parallel", "arbitrary")),
        name="dn_scan",
    )(alog, dtb, eg, eb, q, k, v, ab, q, k, v, ab)


def _diff_attn_kernel(lam_init, q_ref, k_ref, vt_ref, lam_ref, g_ref, ones_ref, o_ref, st_ref):
    i = pl.program_id(1)
    nl = pl.num_programs(1) - 1
    lv = lam_ref[...]
    lam = (jnp.exp(jnp.sum(lv[0:1] * lv[1:2], axis=-1, keepdims=True))
           - jnp.exp(jnp.sum(lv[2:3] * lv[3:4], axis=-1, keepdims=True)) + lam_init)
    lane = lax.broadcasted_iota(jnp.int32, (1, B_QK_W), 1)

    nsm = 2 * DA_HEADS

    def run(groups):
        q = q_ref[...]
        m_prev = None
        res = []
        for n in range(nsm + 1):
            if n < nsm:
                slot = slice(DA_QK * n // LANE * LANE, DA_QK * n // LANE * LANE + LANE)
                lo = DA_QK * n
                qm = jnp.where(jnp.logical_and(lane >= lo, lane < lo + DA_QK), q, jnp.zeros_like(q))[:, slot]
            hp = (n - 1) // 2
            m8 = jnp.full((8, TM), -jnp.inf, F32)
            acc = jnp.zeros((DA_V + 16, TM), F32)
            for grp in groups:
                rows = slice(grp[0] * TM, (grp[-1] + 1) * TM)
                nk = len(grp) * TM
                if n < nsm:
                    st = _dot_nt(k_ref[rows, slot], qm)
                    st_ref[n % 2, rows, :] = st
                    m8 = jnp.maximum(m8, jnp.max(st.reshape(nk // 8, 8, TM), axis=0))
                if n > 0:
                    e = jnp.exp2(st_ref[(n - 1) % 2, rows, :] - m_prev).astype(BF16)
                    vt = jnp.concatenate([vt_ref[c, DA_V * hp:DA_V * hp + DA_V, :] for c in grp], axis=1)
                    lhs = jnp.concatenate([vt, jnp.ones((16, nk), BF16)], axis=0)
                    acc = acc + _dot(lhs, e)
            if n > 0:
                res.append(acc[:DA_V] / acc[DA_V:DA_V + 1])
            if n < nsm:
                m_prev = jnp.max(m8, axis=0, keepdims=True)
        ot = jnp.concatenate([res[2 * h] - lam * res[2 * h + 1] for h in range(DA_HEADS)], axis=0)
        ms = _dot_exact_lhs(ones_ref[...], ot * ot, parts=2)
        yt = (ot * lax.rsqrt(ms + EPS)) * g_ref[...]
        o_ref[...] = (yt * (1.0 - lam_init)).T.astype(BF16)

    ntiles = st_ref.shape[1] // TM

    @pl.when(i < nl)
    def _():
        run([tuple(range(c, min(c + DA_KEY_GROUP, ntiles))) for c in range(0, ntiles, DA_KEY_GROUP)])

    @pl.when(i == nl)
    def _():
        run([(ntiles - 1,)])


def _diff_attn(q, k, vt, lam_vecs, subln_g, ones_b, lam_init):
    nb, ntot, w = q.shape
    nt = ntot // TM
    row = pl.BlockSpec((None, TM, w), lambda b, i: (b, i, 0))
    full = pl.BlockSpec((None, ntot, w), lambda b, i: (b, 0, 0))
    full_t = pl.BlockSpec((None,) + vt.shape[1:], lambda b, i: (b, 0, 0, 0))
    const = lambda a: pl.BlockSpec(a.shape, lambda b, i: (0,) * a.ndim)
    return pl.pallas_call(
        functools.partial(_diff_attn_kernel, lam_init),
        grid=(nb, nt),
        in_specs=[row, full, full_t, const(lam_vecs), const(subln_g), const(ones_b)],
        out_specs=row,
        out_shape=jax.ShapeDtypeStruct((nb, ntot, w), BF16),
        scratch_shapes=[pltpu.VMEM((2, ntot, TM), F32)],
        compiler_params=_params(("parallel", "arbitrary")),
        name="diff_attn",
    )(q, k, vt, lam_vecs, subln_g, ones_b)


def _win_attn_kernel(n_lat, sink_ref, q_ref, k_ref, v_ref, o_ref):
    i = pl.program_id(1)
    nl = pl.num_programs(1) - 1
    rep = WA_HEADS // WA_KV_HEADS
    lane = lax.broadcasted_iota(jnp.int32, (1, LANE), 1)
    lo = lane < WA_DIM
    rowg = lax.broadcasted_iota(jnp.int32, (rep * TM, 1), 0) // TM

    def run(k_all, v_all, bias):
        outs = []
        for g in range(WA_KV_HEADS):
            mine = lo if g == 0 else jnp.logical_not(lo)
            q3 = jnp.concatenate(
                [jnp.where(mine, q_ref[:, LANE * s:LANE * s + LANE], jnp.zeros((TM, LANE), BF16))
                 for s in range(rep)], axis=0)
            s = _dot_nt(q3, k_all)
            if bias is not None:
                s = s + bias
            sk = jnp.zeros((rep * TM, 1), F32)
            for r in range(rep):
                sk = jnp.where(rowg == r, sink_ref[rep * g + r], sk)
            m = jnp.maximum(jnp.max(s, axis=-1, keepdims=True), sk)
            e = jnp.exp(s - m)
            den = jnp.sum(e, axis=-1, keepdims=True) + jnp.exp(sk - m)
            outs.append(_dot(e.astype(BF16), v_all) / den)
        for s in range(rep):
            o_ref[:, LANE * s:LANE * s + LANE] = jnp.where(
                lo, outs[0][TM * s:TM * s + TM], outs[1][TM * s:TM * s + TM]).astype(BF16)

    kc = k_ref[n_lat:, :]
    vc = v_ref[n_lat:, :]
    band = TM + 2 * WINDOW

    @pl.when(i < nl)
    def _():
        start = pl.multiple_of(jnp.clip(i * TM - WINDOW, 0, n_lat - band), WINDOW)
        kb = k_ref[pl.ds(start, band), :]
        vb = v_ref[pl.ds(start, band), :]
        qpos = i * TM + lax.broadcasted_iota(jnp.int32, (TM, 1), 0)
        kpos = start + lax.broadcasted_iota(jnp.int32, (1, band), 1)
        near = jnp.where(jnp.abs(qpos - kpos) <= WINDOW, 0.0, -1e30)
        bias = jnp.concatenate([near, jnp.zeros((TM, kc.shape[0]), F32)], axis=1)
        bias = jnp.concatenate([bias] * rep, axis=0)
        run(jnp.concatenate([kb, kc], axis=0), jnp.concatenate([vb, vc], axis=0), bias)

    @pl.when(i == nl)
    def _():
        run(kc, vc, None)


def _win_attn(q, k, v, sink, n_lat):
    nb, ntot, w = q.shape
    nt = ntot // TM
    kw = k.shape[-1]
    row = pl.BlockSpec((None, TM, w), lambda b, i: (b, i, 0))
    full = pl.BlockSpec((None, ntot, kw), lambda b, i: (b, 0, 0))
    return pl.pallas_call(
        functools.partial(_win_attn_kernel, n_lat),
        grid=(nb, nt),
        in_specs=[pl.BlockSpec(memory_space=pltpu.SMEM), row, full, full],
        out_specs=row,
        out_shape=jax.ShapeDtypeStruct((nb, ntot, w), BF16),
        compiler_params=_params(("parallel", "arbitrary")),
        name="win_attn",
    )(sink, q, k, v)


def _out_proj_kernel(x_ref, of_ref, ob_ref, gate_ref, yb_ref, yw_ref, wa_ref, wb_ref, wc_ref, dng_ref,
                     ones_ref, g1_ref, sc2_ref, sh2_ref, n2_ref, wr_ref, br_ref, xn_o, h2_o, lg_o):
    o = of_ref[...] + ob_ref[...]
    ms = _dot_exact_rhs(o * o, ones_ref[...], parts=2)
    ya = ((o * lax.rsqrt(ms + EPS)) * dng_ref[...]) * _silu(gate_ref[...])
    y = _dot(ya.astype(BF16), wa_ref[...]) + _dot(yb_ref[...], wb_ref[...]) + _dot(yw_ref[...], wc_ref[...])
    xn = x_ref[...] + g1_ref[...] * y
    xn_o[...] = xn
    h2 = _rms_mod(xn, n2_ref[...], sc2_ref[...], sh2_ref[...])
    h2_o[...] = h2.astype(BF16)
    lg_o[...] = _dot3(h2, wr_ref[...]) + br_ref[...]


def _out_proj(x, of, ob, gate, yb, yw, wa, wb, wc, dng, ones_a, mod, n2g, wr, br, nb, ntot):
    d = x.shape[-1]
    nt = ntot // TM
    nl = nt - 1
    row = lambda w: pl.BlockSpec((None, TM, w), lambda b, i: (b, i, 0))
    const = lambda a: pl.BlockSpec(a.shape, lambda b, i: (0,) * a.ndim)
    return pl.pallas_call(
        _out_proj_kernel,
        grid=(nb, nt),
        in_specs=[row(d), row(A_W), row(A_W), row(A_W), row(B_W), row(C_W),
                  const(wa), const(wb), const(wc), const(dng), const(ones_a),
                  _mod_spec_d(2, nb, nl, d), _mod_spec_d(4, nb, nl, d), _mod_spec_d(3, nb, nl, d),
                  const(n2g), const(wr), const(br)],
        out_specs=[row(d), row(d), row(LANE)],
        out_shape=[jax.ShapeDtypeStruct((nb, ntot, d), F32), jax.ShapeDtypeStruct((nb, ntot, d), BF16),
                   jax.ShapeDtypeStruct((nb, ntot, LANE), F32)],
        compiler_params=_params(("parallel", "arbitrary")),
        name="out_proj",
    )(x, of, ob, gate, yb, yw, wa, wb, wc, dng, ones_a, mod, mod, mod, n2g, wr, br)


def _route_kernel(lg_ref, gw_o, pos_o, cnt_o):
    t = lg_ref.shape[0]
    lg = lg_ref[...]
    lane_i = lax.broadcasted_iota(jnp.int32, (1, LANE), 1)
    lane = lane_i.astype(F32)
    big = float(LANE)
    neg = -jnp.inf
    is_g = lane_i < N_GROUPS
    lgm = jnp.where(is_g, lg, neg)
    mg = jnp.max(lgm, axis=-1, keepdims=True)
    p_sel = 1.0 / jnp.sum(jnp.where(is_g, jnp.exp(lgm - mg), 0.0), axis=-1, keepdims=True)
    gidx = jnp.min(jnp.where(jnp.logical_and(is_g, lgm == mg), lane, big), axis=-1, keepdims=True)
    e_lane = lane_i - N_GROUPS
    in_grp = jnp.logical_and(jnp.logical_and(e_lane >= 0, e_lane < N_EXPERTS),
                             jnp.floor((lane - N_GROUPS) * (1.0 / EXP_PER_GROUP)) == gidx)
    le = jnp.where(in_grp, lg, neg)
    v1 = jnp.max(le, axis=-1, keepdims=True)
    i1 = jnp.min(jnp.where(jnp.logical_and(in_grp, le == v1), lane, big), axis=-1, keepdims=True)
    is1 = lane == i1
    le2 = jnp.where(is1, neg, le)
    v2 = jnp.max(le2, axis=-1, keepdims=True)
    rest = jnp.logical_and(in_grp, jnp.logical_not(is1))
    i2 = jnp.min(jnp.where(jnp.logical_and(rest, le2 == v2), lane, big), axis=-1, keepdims=True)
    is2 = lane == i2
    e2 = jnp.exp(v2 - v1)
    w1 = 1.0 / (1.0 + e2)
    w2 = e2 / (1.0 + e2)
    gw = jnp.where(is1, p_sel * w1, jnp.where(is2, p_sel * w2, 0.0))
    sel = jnp.logical_or(jnp.logical_or(is1, is2), lane == gidx)
    self_ = jnp.where(sel, 1.0, 0.0)
    ri = lax.broadcasted_iota(jnp.int32, (ROUTE_BLK, ROUTE_BLK), 0)
    ci = lax.broadcasted_iota(jnp.int32, (ROUTE_BLK, ROUTE_BLK), 1)
    tri = jnp.where(ri > ci, 1.0, 0.0).astype(BF16)
    run = jnp.zeros((1, LANE), F32)
    pos_blocks = []
    for b in range(t // ROUTE_BLK):
        blk = self_[b * ROUTE_BLK:(b + 1) * ROUTE_BLK]
        pos_blocks.append(_dot(tri, blk.astype(BF16)) + run)
        run = run + jnp.sum(blk, axis=0, keepdims=True)
    pos = jnp.where(sel, jnp.concatenate(pos_blocks, axis=0), -1.0)
    gw_o[...] = gw.T
    pos_o[...] = pos.T
    cnt_o[...] = jnp.broadcast_to(run, (8, LANE)).astype(jnp.int32)


def _route(logits):
    ntok = logits.shape[0]
    ntile = ntok // MOE_T
    return pl.pallas_call(
        _route_kernel,
        grid=(ntile,),
        in_specs=[pl.BlockSpec((MOE_T, LANE), lambda t: (t, 0))],
        out_specs=[pl.BlockSpec((None, LANE, MOE_T), lambda t: (t, 0, 0)),
                   pl.BlockSpec((None, LANE, MOE_T), lambda t: (t, 0, 0)),
                   pl.BlockSpec((None, 8, LANE), lambda t: (t, 0, 0))],
        out_shape=[jax.ShapeDtypeStruct((ntile, LANE, MOE_T), F32),
                   jax.ShapeDtypeStruct((ntile, LANE, MOE_T), F32),
                   jax.ShapeDtypeStruct((ntile, 8, LANE), jnp.int32)],
        compiler_params=_params(("parallel",)),
        name="moe_route",
    )(logits)


def _moe_kernel(cnt_ref, h_ref, pos_ref, gw_ref, wg_ref, wu_ref, wd_ref, o_ref, acc_ref):
    t = pl.program_id(0)
    g = pl.program_id(1)

    @pl.when(g == 0)
    def _():
        acc_ref[...] = jnp.zeros_like(acc_ref)

    n = cnt_ref[t * N_GROUPS + g]
    prow = pos_ref[pl.ds(g, 1), :]
    grows = [gw_ref[pl.ds(N_GROUPS + EXP_PER_GROUP * g + e, 1), :] for e in range(EXP_PER_GROUP)]

    def chunk(c, carry):
        slot = (lax.broadcasted_iota(jnp.int32, (MOE_CH, 1), 0) + c * MOE_CH).astype(F32)
        hit = prow == slot
        onehot = jnp.where(hit, 1.0, 0.0).astype(BF16)
        hc = _dot(onehot, h_ref[...]).astype(BF16)
        y = jnp.zeros((MOE_CH, acc_ref.shape[1]), F32)
        for e in range(EXP_PER_GROUP):
            gcol = jnp.sum(jnp.where(hit, grows[e], 0.0), axis=-1, keepdims=True)
            hid = _silu(_dot(hc, wg_ref[e])) * _dot(hc, wu_ref[e])
            y = y + _dot((hid * gcol).astype(BF16), wd_ref[e])
        acc_ref[...] += _dot_tn(onehot, y.astype(BF16))
        return carry

    lax.fori_loop(0, (n + MOE_CH - 1) // MOE_CH, chunk, 0)

    @pl.when(g == pl.num_programs(1) - 1)
    def _():
        o_ref[...] = acc_ref[...].astype(o_ref.dtype)


def _moe(counts, h2, pos_t, gw_t, wg, wu, wd):
    ntok, d = h2.shape
    ntile = ntok // MOE_T
    gs = pltpu.PrefetchScalarGridSpec(
        num_scalar_prefetch=1,
        grid=(ntile, N_GROUPS),
        in_specs=[pl.BlockSpec((MOE_T, d), lambda t, g, c: (t, 0)),
                  pl.BlockSpec((None, LANE, MOE_T), lambda t, g, c: (t, 0, 0)),
                  pl.BlockSpec((None, LANE, MOE_T), lambda t, g, c: (t, 0, 0)),
                  pl.BlockSpec((EXP_PER_GROUP, d, D_EXPERT), lambda t, g, c: (g, 0, 0)),
                  pl.BlockSpec((EXP_PER_GROUP, d, D_EXPERT), lambda t, g, c: (g, 0, 0)),
                  pl.BlockSpec((EXP_PER_GROUP, D_EXPERT, d), lambda t, g, c: (g, 0, 0))],
        out_specs=pl.BlockSpec((MOE_T, d), lambda t, g, c: (t, 0)),
        scratch_shapes=[pltpu.VMEM((MOE_T, d), F32)],
    )
    return pl.pallas_call(
        _moe_kernel,
        grid_spec=gs,
        out_shape=jax.ShapeDtypeStruct((ntok, d), BF16),
        compiler_params=_params(("parallel", "arbitrary")),
        name="moe_experts",
    )(counts, h2, pos_t, gw_t, wg, wu, wd)


def _final_kernel(xn_ref, ff_ref, g2_ref, g_ref, o_ref):
    x = xn_ref[...] + g2_ref[...] * ff_ref[...]
    y = x * lax.rsqrt(jnp.mean(x * x, axis=-1, keepdims=True) + EPS)
    o_ref[...] = y * g_ref[...]


def _final(xn, ff, mod, g, nb, n_lat, ntot):
    d = xn.shape[-1]
    nl = n_lat // TM
    row = pl.BlockSpec((None, TM, d), lambda b, i: (b, i, 0))
    return pl.pallas_call(
        _final_kernel,
        grid=(nb, nl),
        in_specs=[row, row, _mod_spec_d(5, nb, nl, d), pl.BlockSpec(g.shape, lambda b, i: (0, 0))],
        out_specs=row,
        out_shape=jax.ShapeDtypeStruct((nb, n_lat, d), F32),
        compiler_params=_params(("parallel", "arbitrary")),
        name="final_norm",
    )(xn, ff, mod, g)


def _rope_swap_perm(width, dim):
    nf = dim // 4
    j = jnp.arange(width)
    base = (j // (2 * nf)) * (2 * nf)
    return base + (j % (2 * nf) + nf) % (2 * nf)


def _rope_tables(n_lat, n_ctx, dim, width):
    nf = dim // 4
    t = jnp.arange(n_lat)
    row = (t // GRID_W).astype(F32)
    col = (t % GRID_W).astype(F32)
    inv = ROPE_BASE ** (-jnp.arange(nf, dtype=F32) / nf)
    j = jnp.arange(width) % dim
    axis = j // (2 * nf)
    pos = jnp.where(axis[None, :] == 0, row[:, None], col[:, None])
    ang = pos * inv[j % nf][None, :]
    sign = jnp.where(j % (2 * nf) < nf, -1.0, 1.0).astype(F32)
    cos = jnp.concatenate([jnp.cos(ang), jnp.ones((n_ctx, width), F32)], axis=0)
    sin = jnp.concatenate([jnp.sin(ang) * sign[None, :], jnp.zeros((n_ctx, width), F32)], axis=0)
    return cos, sin


def _block_ones(width, group, value):
    j = jnp.arange(width)
    return jnp.where((j[:, None] // group) == (j[None, :] // group), value, 0.0).astype(BF16)


def _slot_cols():
    rep = WA_HEADS // WA_KV_HEADS
    cols = []
    for s in range(rep):
        cols.append(jnp.arange(WA_DIM) + WA_DIM * s)
        cols.append(jnp.arange(WA_DIM) + WA_DIM * (rep + s))
    return jnp.concatenate(cols)


def _build_w_in(w):
    d = w.shape[0]
    s1, s2 = A_COLS, A_COLS + B_COLS
    wa, wb, wc = w[:, :s1], w[:, s1:s2], w[:, s2:]
    ab = jnp.pad(wa[:, QKV_W + A_W:], ((0, 0), (0, LANE - 4 * DN_HEADS)))
    bq, bk, bv = wb[:, :B_QK_W], wb[:, B_QK_W:2 * B_QK_W], wb[:, 2 * B_QK_W:]
    pb = _rope_swap_perm(B_QK_W, DA_QK)
    cq = wc[:, :C_W][:, _slot_cols()]
    ck, cv = wc[:, C_W:C_W + C_KV_W], wc[:, C_W + C_KV_W:]
    pcq = _rope_swap_perm(C_W, WA_DIM)
    pck = _rope_swap_perm(C_KV_W, WA_DIM)
    cat = jnp.concatenate([wa[:, :QKV_W], wa[:, QKV_W:QKV_W + A_W], ab,
                           bq, bk, bv, bq[:, pb], bk[:, pb],
                           cq, ck, cv, cq[:, pcq], ck[:, pck]], axis=1)
    assert cat.shape == (d, _O_END)
    return cat.astype(BF16)


def _expand_mats():
    r = jnp.arange(LANE)[:, None]
    h = (jnp.arange(A_QK_W) // DN_DK)[None, :]
    eg = jnp.stack([(r == DN_HEADS * d + h) for d in range(2)]).astype(BF16)
    eb = jnp.stack([(r == 2 * DN_HEADS + DN_HEADS * d + h) for d in range(2)]).astype(BF16)
    return eg, eb


def kernel(x, c, ctx, c_ctx, ada_w, ada_b, norm1_g, norm2_g, w_in, dn_conv_w, dn_a_log, dn_dt_bias, dn_norm_g, da_lambda, da_subln_g, wa_sink, w_out, router_group_w, router_group_b, router_expert_w, router_expert_b, exp_w_gate, exp_w_up, exp_w_down, final_norm_g):
    nb, n_lat, d = x.shape
    n_ctx = ctx.shape[1]
    depth = ada_w.shape[0]
    assert n_ctx == TM and n_lat % TM == 0 and n_lat >= 3 * TM
    ntot = n_lat + n_ctx
    ntok = nb * ntot
    assert ntok % MOE_T == 0

    cc = jnp.zeros((16, d), F32).at[:nb].set(c).at[nb].set(c_ctx)
    cosb, sinb = _rope_tables(n_lat, n_ctx, DA_QK, B_QK_W)
    cosc, sinc = _rope_tables(n_lat, n_ctx, WA_DIM, C_KV_W)
    tabs = (cosb, sinb, cosc, sinc)
    ones_a = _block_ones(A_W, DN_DV, 1.0)
    mean_a = _block_ones(A_W, DN_DV, 1.0 / DN_DV)
    mean_b = _block_ones(B_W, DA_V, 1.0 / DA_V)
    eg, eb = _expand_mats()
    slot_rows = _slot_cols()
    pad_lane = lambda v: jnp.pad(v.reshape(1, -1), ((0, 0), (0, LANE - v.size)))

    mods = [_ada(cc, ada_w[li], ada_b[li]).reshape(16, 6, 1, d) for li in range(depth)]
    xn = ff = None
    for li in range(depth):
        mod = mods[li]
        w_cat = _build_w_in(w_in[li])
        n1g = norm1_g[li].reshape(1, d)
        if li == 0:
            outs = _in_proj(("split", x, ctx), (mod,), n1g, w_cat, tabs, nb, ntot)
        else:
            outs = _in_proj(("res", xn, ff.reshape(nb, ntot, d)), (mod, mods[li - 1]), n1g, w_cat, tabs, nb, ntot)
        xcur, outs = outs[0], outs[1:]
        zqkv, gate, ab, qb, kb, vb, qc, kc, vc = outs

        conv_w8 = jnp.pad(dn_conv_w[li], ((0, 8 - DN_CONV), (0, 0)))
        q, k, v = _dn_prep(zqkv, conv_w8, ones_a)
        of, ob = _dn_scan(q, k, v, ab, pad_lane(dn_a_log[li]), pad_lane(dn_dt_bias[li]), eg, eb)

        lam_init = 0.8 - 0.6 * math.exp(-0.3 * li)
        yb = _diff_attn(qb, kb, vb, da_lambda[li], jnp.tile(da_subln_g[li], DA_HEADS).reshape(B_W, 1),
                        mean_b, lam_init)
        yw = _win_attn(qc, kc, vc, jnp.pad(wa_sink[li], (0, 8 - WA_HEADS)), n_lat)

        wo = w_out[li]
        wa_o = wo[:A_W].astype(BF16)
        wb_o = wo[A_W:A_W + B_W].astype(BF16)
        wc_o = wo[A_W + B_W:][slot_rows].astype(BF16)
        wr = jnp.pad(jnp.concatenate([router_group_w[li], router_expert_w[li]], axis=1),
                     ((0, 0), (0, LANE - N_GROUPS - N_EXPERTS)))
        br = pad_lane(jnp.concatenate([router_group_b[li], router_expert_b[li]]))
        xn, h2, logits = _out_proj(xcur, of, ob, gate, yb, yw, wa_o, wb_o, wc_o,
                                   jnp.tile(dn_norm_g[li], DN_HEADS).reshape(1, A_W), mean_a, mod,
                                   norm2_g[li].reshape(1, d), wr, br, nb, ntot)

        gw_t, pos_t, cnt = _route(logits.reshape(ntok, LANE))
        counts = cnt[:, 0, :N_GROUPS].reshape(-1)
        ff = _moe(counts, h2.reshape(ntok, d), pos_t, gw_t, exp_w_gate[li].astype(BF16),
                  exp_w_up[li].astype(BF16), exp_w_down[li].astype(BF16))

    return _final(xn, ff.reshape(nb, ntot, d), mods[depth - 1], final_norm_g.reshape(1, d), nb, n_lat, ntot)
```

```python
import functools
import math

import jax
import jax.numpy as jnp
from jax import lax
from jax.experimental import pallas as pl
from jax.experimental.pallas import tpu as pltpu

F32 = jnp.float32
BF16 = jnp.bfloat16

GRID_W = 64
EPS = 1e-6
LOG2E = math.log2(math.e)
ROPE_BASE = 10000.0
DN_HEADS = 6
DN_DK = 64
DN_DV = 64
DN_CONV = 5
DN_CHUNK = 64
DA_HEADS = 4
DA_QK = 32
DA_V = 64
WA_HEADS = 6
WA_KV_HEADS = 2
WA_DIM = 64
WINDOW = 128
N_GROUPS = 4
EXP_PER_GROUP = 4
N_EXPERTS = 16
D_EXPERT = 512

A_QK_W = DN_HEADS * DN_DK
A_W = DN_HEADS * DN_DV
QKV_W = 2 * A_QK_W + A_W
B_W = DA_HEADS * DA_V
B_QK_W = 2 * DA_HEADS * DA_QK
C_W = WA_HEADS * WA_DIM
C_KV_W = WA_KV_HEADS * WA_DIM
A_COLS = QKV_W + A_W + 4 * DN_HEADS
B_COLS = 2 * B_QK_W + B_W

LANE = 128
TM = 256
MOE_T = 1024
MOE_SUB = 2
MOE_CH = 288
ROUTE_BLK = 256
DN_EXACT_ROUNDS = 5
DA_KEY_GROUP = 8
VMEM_LIMIT = 56 * 1024 * 1024

_O_QKV = 0
_O_GATE = _O_QKV + QKV_W
_O_AB = _O_GATE + A_W
_O_BQ = _O_AB + LANE
_O_BK = _O_BQ + B_QK_W
_O_BV = _O_BK + B_QK_W
_O_BQS = _O_BV + B_W
_O_BKS = _O_BQS + B_QK_W
_O_CQ = _O_BKS + B_QK_W
_O_CK = _O_CQ + C_W
_O_CV = _O_CK + C_KV_W
_O_CQS = _O_CV + C_KV_W
_O_CKS = _O_CQS + C_W
_O_END = _O_CKS + C_KV_W


def _dot(a, b):
    return jnp.dot(a, b, preferred_element_type=F32)


def _dot_nt(a, b):
    return lax.dot_general(a, b, (((1,), (1,)), ((), ())), preferred_element_type=F32)


def _dot_tn(a, b):
    return lax.dot_general(a, b, (((0,), (0,)), ((), ())), preferred_element_type=F32)


def _split2(a):
    hi = a.astype(BF16)
    lo = (a - hi.astype(F32)).astype(BF16)
    return hi, lo


def _split3(a):
    hi = a.astype(BF16)
    r = a - hi.astype(F32)
    mid = r.astype(BF16)
    lo = (r - mid.astype(F32)).astype(BF16)
    return hi, mid, lo


def _dot3(a, b):
    ah, al = _split2(a)
    bh, bl = _split2(b)
    return _dot(ah, bh) + (_dot(ah, bl) + _dot(al, bh))


def _dot_exact_rhs(a, b_bf16, parts=3):
    sp = _split3(a) if parts == 3 else _split2(a)
    out = _dot(sp[0], b_bf16)
    for p in sp[1:]:
        out = out + _dot(p, b_bf16)
    return out


def _dot_exact_lhs(a_bf16, b, parts=3):
    sp = _split3(b) if parts == 3 else _split2(b)
    out = _dot(a_bf16, sp[0])
    for p in sp[1:]:
        out = out + _dot(a_bf16, p)
    return out


def _col_reduce(x, op, slab=64):
    n, w = x.shape
    if n > slab and n % slab == 0:
        x = op(x.reshape(n // slab, slab, w), axis=0)
    return op(x, axis=0, keepdims=True)


def _silu(x):
    return x * jax.nn.sigmoid(x)


def _softplus(x):
    return jnp.maximum(x, 0.0) + jnp.log1p(jnp.exp(-jnp.abs(x)))


def _params(sem):
    return pltpu.CompilerParams(dimension_semantics=sem, vmem_limit_bytes=VMEM_LIMIT)


def _ada_kernel(c_ref, w_ref, b_ref, o_ref):
    o_ref[...] = _dot3(_silu(c_ref[...]), w_ref[...]) + b_ref[...]


def _ada(cc, w, b):
    rows, d = cc.shape
    n = w.shape[1]
    tn = n // 4
    return pl.pallas_call(
        _ada_kernel,
        grid=(n // tn,),
        in_specs=[pl.BlockSpec((rows, d), lambda j: (0, 0)),
                  pl.BlockSpec((d, tn), lambda j: (0, j)),
                  pl.BlockSpec((1, tn), lambda j: (0, j))],
        out_specs=pl.BlockSpec((rows, tn), lambda j: (0, j)),
        out_shape=jax.ShapeDtypeStruct((rows, n), F32),
        compiler_params=_params(("arbitrary",)),
        name="ada_mod",
    )(cc, w, b.reshape(1, n))


def _mod_spec_d(k, nb, nl, d):
    return pl.BlockSpec((None, None, 1, d), lambda b, i: (jnp.where(i == nl, nb, b), k, 0, 0))


def _rms_mod(x, g, sc, sh):
    y = x * lax.rsqrt(jnp.mean(x * x, axis=-1, keepdims=True) + EPS)
    return (y * g) * (1.0 + sc) + sh


def _in_proj_kernel(fuse_res, *refs):
    if fuse_res:
        xn_ref, ff_ref, g2_ref = refs[:3]
        refs = refs[3:]
    else:
        xl_ref, xc_ref = refs[:2]
        refs = refs[2:]
    (sc_ref, sh_ref, g_ref, w_ref, cosb_ref, sinb_ref, cosc_ref, sinc_ref) = refs[:8]
    outs = refs[8:]
    if fuse_res:
        x = xn_ref[...] + g2_ref[...] * ff_ref[...]
    else:
        is_ctx = pl.program_id(1) == pl.num_programs(1) - 1
        x = jnp.where(is_ctx, xc_ref[...], xl_ref[...])
    outs[0][...] = x
    outs = outs[1:]
    (zqkv_o, gate_o, ab_o, qb_o, kb_o, vb_o, qc_o, kc_o, vc_o) = outs
    hb = _rms_mod(x, g_ref[...], sc_ref[...], sh_ref[...]).astype(BF16)

    def seg(a, b):
        return _dot(hb, w_ref[:, a:b])

    zqkv_o[...] = seg(_O_QKV, _O_GATE)
    gate_o[...] = seg(_O_GATE, _O_AB)
    ab_o[...] = seg(_O_AB, _O_BQ)
    cb = cosb_ref[...]
    sb = sinb_ref[...]
    qb_o[...] = ((seg(_O_BQ, _O_BK) * cb + seg(_O_BQS, _O_BKS) * sb) * (DA_QK ** -0.5 * LOG2E)).astype(BF16)
    kb_o[...] = (seg(_O_BK, _O_BV) * cb + seg(_O_BKS, _O_CQ) * sb).astype(BF16)
    vb_o[...] = seg(_O_BV, _O_BQS).T.astype(BF16)
    cc = cosc_ref[...]
    sc_ = sinc_ref[...]
    cc3 = jnp.concatenate([cc, cc, cc], axis=1)
    sc3 = jnp.concatenate([sc_, sc_, sc_], axis=1)
    qc_o[...] = ((seg(_O_CQ, _O_CK) * cc3 + seg(_O_CQS, _O_CKS) * sc3) * (WA_DIM ** -0.5)).astype(BF16)
    kc_o[...] = (seg(_O_CK, _O_CV) * cc + seg(_O_CKS, _O_END) * sc_).astype(BF16)
    vc_o[...] = seg(_O_CV, _O_CQS).astype(BF16)


def _in_proj(x_parts, mod, norm_g, w_cat, tabs, nb, ntot):
    fuse_res = x_parts[0] == "res"
    x_parts = x_parts[1:]
    d = x_parts[0].shape[-1]
    nt = ntot // TM
    nl = nt - 1
    row = lambda w: pl.BlockSpec((None, TM, w), lambda b, i: (b, i, 0))
    tab = lambda w: pl.BlockSpec((TM, w), lambda b, i: (i, 0))
    const = lambda a: pl.BlockSpec(a.shape, lambda b, i: (0,) * a.ndim)
    if fuse_res:
        (xn, ff), prev_mod = x_parts, mod[1]
        ins = [xn, ff, prev_mod]
        in_specs = [row(d), row(d), _mod_spec_d(5, nb, nl, d)]
        cur_mod = mod[0]
    else:
        ins = list(x_parts)
        in_specs = [pl.BlockSpec((None, TM, d), lambda b, i: (b, jnp.minimum(i, nl - 1), 0)),
                    pl.BlockSpec((None, TM, d), lambda b, i: (b, 0, 0))]
        cur_mod = mod[0]
    ins += [cur_mod, cur_mod, norm_g, w_cat, *tabs]
    in_specs += [_mod_spec_d(1, nb, nl, d), _mod_spec_d(0, nb, nl, d), const(norm_g), const(w_cat),
                 tab(B_QK_W), tab(B_QK_W), tab(C_KV_W), tab(C_KV_W)]
    widths = [(QKV_W, F32), (A_W, F32), (LANE, F32), (B_QK_W, BF16), (B_QK_W, BF16), (None, BF16),
              (C_W, BF16), (C_KV_W, BF16), (C_KV_W, BF16)]
    widths = [(d, F32)] + widths
    return pl.pallas_call(
        functools.partial(_in_proj_kernel, fuse_res),
        grid=(nb, nt),
        in_specs=in_specs,
        out_specs=[pl.BlockSpec((None, None, B_W, TM), lambda b, i: (b, i, 0, 0)) if w is None else row(w)
                   for w, _ in widths],
        out_shape=[jax.ShapeDtypeStruct((nb, nt, B_W, TM) if w is None else (nb, ntot, w), dt)
                   for w, dt in widths],
        compiler_params=_params(("parallel", "arbitrary")),
        name="in_proj",
    )(*ins)


def _dn_prep_kernel(zc_ref, zp_ref, zn_ref, w_ref, ones_ref, q_o, k_o, v_o, ext_ref):
    i = pl.program_id(1)
    nl = pl.num_programs(1) - 1
    prev_ok = jnp.logical_and(i >= 1, i <= nl - 1)
    next_ok = i <= nl - 2
    ext_ref[0:8, :] = jnp.where(prev_ok, zp_ref[...], 0.0)
    ext_ref[8:8 + TM, :] = zc_ref[...]
    ext_ref[8 + TM:16 + TM, :] = jnp.where(next_ok, zn_ref[...], 0.0)
    half = DN_CONV // 2
    acc = w_ref[0:1, :] * ext_ref[8 - half:8 - half + TM, :]
    for j in range(1, DN_CONV):
        acc = acc + w_ref[j:j + 1, :] * ext_ref[8 - half + j:8 - half + j + TM, :]
    y = _silu(acc)
    ones = ones_ref[...]

    def l2n(t):
        ss = _dot_exact_rhs(t * t, ones, parts=2)
        return t * lax.rsqrt(ss + EPS)

    q_o[...] = l2n(y[:, :A_QK_W]) * (DN_DK ** -0.5)
    k_o[...] = l2n(y[:, A_QK_W:2 * A_QK_W])
    v_o[...] = y[:, 2 * A_QK_W:]


def _dn_prep(zqkv, conv_w8, ones_a):
    nb, ntot, w = zqkv.shape
    nt = ntot // TM
    r8 = TM // 8
    row = lambda ww: pl.BlockSpec((None, TM, ww), lambda b, i: (b, i, 0))
    return pl.pallas_call(
        _dn_prep_kernel,
        grid=(nb, nt),
        in_specs=[row(w),
                  pl.BlockSpec((None, 8, w), lambda b, i: (b, jnp.maximum(i * r8 - 1, 0), 0)),
                  pl.BlockSpec((None, 8, w), lambda b, i: (b, jnp.minimum(i * r8 + r8, ntot // 8 - 1), 0)),
                  pl.BlockSpec(conv_w8.shape, lambda b, i: (0, 0)),
                  pl.BlockSpec(ones_a.shape, lambda b, i: (0, 0))],
        out_specs=[row(A_QK_W), row(A_QK_W), row(A_W)],
        out_shape=[jax.ShapeDtypeStruct((nb, ntot, A_QK_W), F32)] * 3,
        scratch_shapes=[pltpu.VMEM((TM + 16, w), F32)],
        compiler_params=_params(("parallel", "arbitrary")),
        name="dn_prep",
    )(zqkv, zqkv, zqkv, conv_w8, ones_a)


def _dn_pre(d, rows, q_ref, k_ref, v_ref, ab_ref, alog, dtb, eg_ref, eb_ref):
    c = DN_CHUNK
    q = q_ref[rows, :]
    k = k_ref[rows, :]
    v = v_ref[rows, :]
    ab = ab_ref[rows, :]
    g = -jnp.exp(alog) * _softplus(ab + dtb)
    beta = jax.nn.sigmoid(ab)
    ri = lax.broadcasted_iota(jnp.int32, (c, 2 * c), 0)
    ci = lax.broadcasted_iota(jnp.int32, (c, 2 * c), 1) % c
    if d == 0:
        incl, strict = ri >= ci, ri > ci
    else:
        incl, strict = ri <= ci, ri < ci
    cum = jnp.where(incl[:, :c], 1.0, 0.0).astype(BF16)
    gc = _dot_exact_lhs(cum, g)
    last = c - 1 if d == 0 else 0
    g_last = gc[last:last + 1, :]
    egc = jnp.exp(gc)
    ekd = jnp.exp(g_last - gc)
    egl = jnp.broadcast_to(jnp.exp(g_last), (8, LANE))
    eg = eg_ref[d]
    eb = eb_ref[d]
    beta_x = _dot_exact_rhs(beta, eb)
    gx = _dot_exact_rhs(jnp.concatenate([egc, ekd, gc, egl], axis=0), eg)
    egc_x, ekd_x, gc_x, egl_x = gx[0:c], gx[c:2 * c], gx[2 * c:3 * c], gx[3 * c:3 * c + 1]
    gc_t = jnp.concatenate([gc, gc], axis=0).T
    kbeta = k * beta_x
    vbeta = v * beta_x
    wrhs = kbeta * egc_x
    qd = q * egc_x
    kd = k * ekd_x
    lane = lax.broadcasted_iota(jnp.int32, (1, LANE), 1)
    lo = lane < DN_DK
    pairs = []
    for p in range(DN_HEADS // 2):
        sl = slice(LANE * p, LANE * p + LANE)
        k_s, q_s = k[:, sl], q[:, sl]
        kb_s, vb_s, wr_s = kbeta[:, sl], vbeta[:, sl], wrhs[:, sl]
        gcx_s = gc_x[:, sl]
        k_sb = k_s.astype(BF16)
        k_sb2 = jnp.concatenate([k_sb, k_sb], axis=0)
        mats, rhss, attns = [], [], []
        for j in range(2):
            h = 2 * p + j
            mine = lo if j == 0 else jnp.logical_not(lo)
            gcx_r = pltpu.roll(gcx_s, DN_DK, 1)
            gcol = jnp.where(lo, gcx_s, gcx_r) if j == 0 else jnp.where(lo, gcx_r, gcx_s)
            grow = gc_t[DN_HEADS * d + h:DN_HEADS * d + h + 1, :]
            diff = gcol - grow
            dec = jnp.where(incl, jnp.exp(jnp.where(incl, diff, 0.0)), 0.0)
            kk = _dot_nt(jnp.where(mine, kb_s, 0.0).astype(BF16), k_sb2)
            mats.append(jnp.where(strict, kk * dec, 0.0))
            qk = _dot_nt(jnp.where(mine, q_s, 0.0).astype(BF16), k_sb)
            attns.append((qk * dec[:, :c]).astype(BF16))
            if j == 0:
                rhss.append(jnp.where(lo, vb_s, pltpu.roll(wr_s, DN_DK, 1)))
            else:
                rhss.append(jnp.where(lo, pltpu.roll(vb_s, DN_DK, 1), wr_s))
        pairs.append(dict(mats=mats, rhss=rhss, attns=attns, qd=qd[:, sl].astype(BF16),
                          kd_t=kd[:, sl].T.astype(BF16), egl=egl_x[:, sl]))
    return pairs


def _dn_solve(mats, rhss):
    c = DN_CHUNK
    lo = lax.broadcasted_iota(jnp.int32, (1, 2 * c), 1) < c
    pw, xs = list(mats), list(rhss)
    rounds = 6
    for r in range(rounds):
        for n in range(len(pw)):
            last = r == rounds - 1
            ph = pw[n].astype(BF16)
            xh = xs[n].astype(BF16)
            bh = xh if last else jnp.concatenate([xh, ph], axis=1)
            if r < DN_EXACT_ROUNDS:
                plo = (pw[n] - ph.astype(F32)).astype(BF16)
                xl = (xs[n] - xh.astype(F32)).astype(BF16)
                bl = xl if last else jnp.concatenate([xl, plo], axis=1)
                lhs = jnp.concatenate([jnp.where(lo, ph, plo), ph[:, :c]], axis=1)
                both = _dot(lhs, jnp.concatenate([bh, bh, bl], axis=0))
            else:
                both = _dot(ph[:, :c], bh)
            px = both[:, :2 * c]
            if r < rounds - 1:
                pw[n] = both[:, 2 * c:]
            xs[n] = xs[n] - px if r == 0 else xs[n] + px
    return xs


def _dn_scan_kernel(alog_ref, dtb_ref, eg_ref, eb_ref,
                    qf, kf, vf, abf, qb, kb, vb, abb, of_ref, ob_ref, s_ref):
    i = pl.program_id(1)

    @pl.when(i == 0)
    def _():
        s_ref[...] = jnp.zeros_like(s_ref)

    alog = alog_ref[...]
    dtb = dtb_ref[...]
    nch = TM // DN_CHUNK
    npair = DN_HEADS // 2
    lane = lax.broadcasted_iota(jnp.int32, (1, LANE), 1)
    lo = lane < DN_DK
    ri2 = lax.broadcasted_iota(jnp.int32, (LANE, LANE), 0)
    ci2 = lax.broadcasted_iota(jnp.int32, (LANE, LANE), 1)
    bdiag = (ri2 < DN_DK) == (ci2 < DN_DK)

    pairs, rows_of = [], []
    for g in range(nch):
        rf = slice(g * DN_CHUNK, (g + 1) * DN_CHUNK)
        rb = slice((nch - 1 - g) * DN_CHUNK, (nch - g) * DN_CHUNK)
        pairs += (_dn_pre(0, rf, qf, kf, vf, abf, alog, dtb, eg_ref, eb_ref)
                  + _dn_pre(1, rb, qb, kb, vb, abb, alog, dtb, eg_ref, eb_ref))
        rows_of += [rf] * npair + [rb] * npair
    xs = _dn_solve([m for pr in pairs for m in pr["mats"]], [r for pr in pairs for r in pr["rhss"]])
    for m, pr in enumerate(pairs):
        n = m % (2 * npair)
        d, p = divmod(n, npair)
        x0, x1 = xs[2 * m], xs[2 * m + 1]
        u = jnp.where(lo, x0, pltpu.roll(x1, DN_DK, 1))
        w = jnp.where(lo, pltpu.roll(x0, DN_DK, 1), x1)
        s = s_ref[n]
        sb = s.astype(BF16)
        v_new = u - _dot(w.astype(BF16), sb)
        vn_b = v_new.astype(BF16)
        o = _dot(pr["qd"], sb)
        o = o + _dot(pr["attns"][0], jnp.where(lo, vn_b, jnp.zeros_like(vn_b)))
        o = o + _dot(pr["attns"][1], jnp.where(lo, jnp.zeros_like(vn_b), vn_b))
        upd = _dot(pr["kd_t"], vn_b)
        s_ref[n] = s * pr["egl"] + jnp.where(bdiag, upd, 0.0)
        o_ref = of_ref if d == 0 else ob_ref
        o_ref[rows_of[m], LANE * p:LANE * p + LANE] = o


def _dn_scan(q, k, v, ab, alog, dtb, eg, eb):
    nb, ntot, w = q.shape
    nt = ntot // TM
    nl = nt - 1
    fwd = lambda b, i: (b, jnp.where(i == 0, nl, i - 1), 0)
    bwd = lambda b, i: (b, jnp.where(i == 0, nl, nl - i), 0)
    const = lambda a: pl.BlockSpec(a.shape, lambda b, i: (0,) * a.ndim)
    blk = lambda ww, im: pl.BlockSpec((None, TM, ww), im)
    return pl.pallas_call(
        _dn_scan_kernel,
        grid=(nb, nt),
        in_specs=[const(alog), const(dtb), const(eg), const(eb),
                  blk(w, fwd), blk(w, fwd), blk(w, fwd), blk(LANE, fwd),
                  blk(w, bwd), blk(w, bwd), blk(w, bwd), blk(LANE, bwd)],
        out_specs=[blk(w, fwd), blk(w, bwd)],
        out_shape=[jax.ShapeDtypeStruct((nb, ntot, w), F32)] * 2,
        scratch_shapes=[pltpu.VMEM((2 * (DN_HEADS // 2), LANE, LANE), F32)],
        compiler_params=_params(("parallel", "arbitrary")),
        name="dn_scan",
    )(alog, dtb, eg, eb, q, k, v, ab, q, k, v, ab)


def _diff_attn_kernel(lam_init, q_ref, k_ref, vt_ref, lam_ref, g_ref, ones_ref, o_ref, st_ref):
    i = pl.program_id(1)
    nl = pl.num_programs(1) - 1
    lv = lam_ref[...]
    lam = (jnp.exp(jnp.sum(lv[0:1] * lv[1:2], axis=-1, keepdims=True))
           - jnp.exp(jnp.sum(lv[2:3] * lv[3:4], axis=-1, keepdims=True)) + lam_init)
    lane = lax.broadcasted_iota(jnp.int32, (1, B_QK_W), 1)

    nsm = 2 * DA_HEADS

    def run(groups):
        q = q_ref[...]
        m_prev = None
        res = []
        for n in range(nsm + 1):
            if n < nsm:
                slot = slice(DA_QK * n // LANE * LANE, DA_QK * n // LANE * LANE + LANE)
                lo = DA_QK * n
                qm = jnp.where(jnp.logical_and(lane >= lo, lane < lo + DA_QK), q, jnp.zeros_like(q))[:, slot]
            hp = (n - 1) // 2
            m8 = jnp.full((8, TM), -jnp.inf, F32)
            acc = jnp.zeros((DA_V + 16, TM), F32)
            for grp in groups:
                rows = slice(grp[0] * TM, (grp[-1] + 1) * TM)
                nk = len(grp) * TM
                if n < nsm:
                    st = _dot_nt(k_ref[rows, slot], qm)
                    st_ref[n % 2, rows, :] = st
                    m8 = jnp.maximum(m8, jnp.max(st.reshape(nk // 8, 8, TM), axis=0))
                if n > 0:
                    e = jnp.exp2(st_ref[(n - 1) % 2, rows, :] - m_prev).astype(BF16)
                    vt = jnp.concatenate([vt_ref[c, DA_V * hp:DA_V * hp + DA_V, :] for c in grp], axis=1)
                    lhs = jnp.concatenate([vt, jnp.ones((16, nk), BF16)], axis=0)
                    acc = acc + _dot(lhs, e)
            if n > 0:
                res.append(acc[:DA_V] / acc[DA_V:DA_V + 1])
            if n < nsm:
                m_prev = jnp.max(m8, axis=0, keepdims=True)
        ot = jnp.concatenate([res[2 * h] - lam * res[2 * h + 1] for h in range(DA_HEADS)], axis=0)
        ms = _dot_exact_lhs(ones_ref[...], ot * ot, parts=2)
        yt = (ot * lax.rsqrt(ms + EPS)) * g_ref[...]
        o_ref[...] = (yt * (1.0 - lam_init)).T.astype(BF16)

    ntiles = st_ref.shape[1] // TM

    @pl.when(i < nl)
    def _():
        run([tuple(range(c, min(c + DA_KEY_GROUP, ntiles))) for c in range(0, ntiles, DA_KEY_GROUP)])

    @pl.when(i == nl)
    def _():
        run([(ntiles - 1,)])


def _diff_attn(q, k, vt, lam_vecs, subln_g, ones_b, lam_init):
    nb, ntot, w = q.shape
    nt = ntot // TM
    row = pl.BlockSpec((None, TM, w), lambda b, i: (b, i, 0))
    full = pl.BlockSpec((None, ntot, w), lambda b, i: (b, 0, 0))
    full_t = pl.BlockSpec((None,) + vt.shape[1:], lambda b, i: (b, 0, 0, 0))
    const = lambda a: pl.BlockSpec(a.shape, lambda b, i: (0,) * a.ndim)
    return pl.pallas_call(
        functools.partial(_diff_attn_kernel, lam_init),
        grid=(nb, nt),
        in_specs=[row, full, full_t, const(lam_vecs), const(subln_g), const(ones_b)],
        out_specs=row,
        out_shape=jax.ShapeDtypeStruct((nb, ntot, w), BF16),
        scratch_shapes=[pltpu.VMEM((2, ntot, TM), F32)],
        compiler_params=_params(("parallel", "arbitrary")),
        name="diff_attn",
    )(q, k, vt, lam_vecs, subln_g, ones_b)


def _win_attn_kernel(n_lat, sink_ref, q_ref, k_ref, v_ref, o_ref):
    i = pl.program_id(1)
    nl = pl.num_programs(1) - 1
    rep = WA_HEADS // WA_KV_HEADS
    lane = lax.broadcasted_iota(jnp.int32, (1, LANE), 1)
    lo = lane < WA_DIM
    rowg = lax.broadcasted_iota(jnp.int32, (rep * TM, 1), 0) // TM

    def run(k_all, v_all, bias):
        outs = []
        for g in range(WA_KV_HEADS):
            mine = lo if g == 0 else jnp.logical_not(lo)
            q3 = jnp.concatenate(
                [jnp.where(mine, q_ref[:, LANE * s:LANE * s + LANE], jnp.zeros((TM, LANE), BF16))
                 for s in range(rep)], axis=0)
            s = _dot_nt(q3, k_all)
            if bias is not None:
                s = s + bias
            sk = jnp.zeros((rep * TM, 1), F32)
            for r in range(rep):
                sk = jnp.where(rowg == r, sink_ref[rep * g + r], sk)
            m = jnp.maximum(jnp.max(s, axis=-1, keepdims=True), sk)
            e = jnp.exp(s - m)
            den = jnp.sum(e, axis=-1, keepdims=True) + jnp.exp(sk - m)
            outs.append(_dot(e.astype(BF16), v_all) / den)
        for s in range(rep):
            o_ref[:, LANE * s:LANE * s + LANE] = jnp.where(
                lo, outs[0][TM * s:TM * s + TM], outs[1][TM * s:TM * s + TM]).astype(BF16)

    kc = k_ref[n_lat:, :]
    vc = v_ref[n_lat:, :]
    band = TM + 2 * WINDOW

    @pl.when(i < nl)
    def _():
        start = pl.multiple_of(jnp.clip(i * TM - WINDOW, 0, n_lat - band), WINDOW)
        kb = k_ref[pl.ds(start, band), :]
        vb = v_ref[pl.ds(start, band), :]
        qpos = i * TM + lax.broadcasted_iota(jnp.int32, (TM, 1), 0)
        kpos = start + lax.broadcasted_iota(jnp.int32, (1, band), 1)
        near = jnp.where(jnp.abs(qpos - kpos) <= WINDOW, 0.0, -1e30)
        bias = jnp.concatenate([near, jnp.zeros((TM, kc.shape[0]), F32)], axis=1)
        bias = jnp.concatenate([bias] * rep, axis=0)
        run(jnp.concatenate([kb, kc], axis=0), jnp.concatenate([vb, vc], axis=0), bias)

    @pl.when(i == nl)
    def _():
        run(kc, vc, None)


def _win_attn(q, k, v, sink, n_lat):
    nb, ntot, w = q.shape
    nt = ntot // TM
    kw = k.shape[-1]
    row = pl.BlockSpec((None, TM, w), lambda b, i: (b, i, 0))
    full = pl.BlockSpec((None, ntot, kw), lambda b, i: (b, 0, 0))
    return pl.pallas_call(
        functools.partial(_win_attn_kernel, n_lat),
        grid=(nb, nt),
        in_specs=[pl.BlockSpec(memory_space=pltpu.SMEM), row, full, full],
        out_specs=row,
        out_shape=jax.ShapeDtypeStruct((nb, ntot, w), BF16),
        compiler_params=_params(("parallel", "arbitrary")),
        name="win_attn",
    )(sink, q, k, v)


def _out_proj_kernel(x_ref, of_ref, ob_ref, gate_ref, yb_ref, yw_ref, wa_ref, wb_ref, wc_ref, dng_ref,
                     ones_ref, g1_ref, sc2_ref, sh2_ref, n2_ref, wr_ref, br_ref, xn_o, h2_o, lg_o):
    o = of_ref[...] + ob_ref[...]
    ms = _dot_exact_rhs(o * o, ones_ref[...], parts=2)
    ya = ((o * lax.rsqrt(ms + EPS)) * dng_ref[...]) * _silu(gate_ref[...])
    y = _dot(ya.astype(BF16), wa_ref[...]) + _dot(yb_ref[...], wb_ref[...]) + _dot(yw_ref[...], wc_ref[...])
    xn = x_ref[...] + g1_ref[...] * y
    xn_o[...] = xn
    h2 = _rms_mod(xn, n2_ref[...], sc2_ref[...], sh2_ref[...])
    h2_o[...] = h2.astype(BF16)
    lg_o[...] = _dot3(h2, wr_ref[...]) + br_ref[...]


def _out_proj(x, of, ob, gate, yb, yw, wa, wb, wc, dng, ones_a, mod, n2g, wr, br, nb, ntot):
    d = x.shape[-1]
    nt = ntot // TM
    nl = nt - 1
    row = lambda w: pl.BlockSpec((None, TM, w), lambda b, i: (b, i, 0))
    const = lambda a: pl.BlockSpec(a.shape, lambda b, i: (0,) * a.ndim)
    return pl.pallas_call(
        _out_proj_kernel,
        grid=(nb, nt),
        in_specs=[row(d), row(A_W), row(A_W), row(A_W), row(B_W), row(C_W),
                  const(wa), const(wb), const(wc), const(dng), const(ones_a),
                  _mod_spec_d(2, nb, nl, d), _mod_spec_d(4, nb, nl, d), _mod_spec_d(3, nb, nl, d),
                  const(n2g), const(wr), const(br)],
        out_specs=[row(d), row(d), row(LANE)],
        out_shape=[jax.ShapeDtypeStruct((nb, ntot, d), F32), jax.ShapeDtypeStruct((nb, ntot, d), BF16),
                   jax.ShapeDtypeStruct((nb, ntot, LANE), F32)],
        compiler_params=_params(("parallel", "arbitrary")),
        name="out_proj",
    )(x, of, ob, gate, yb, yw, wa, wb, wc, dng, ones_a, mod, mod, mod, n2g, wr, br)


def _route_kernel(lg_ref, gw_o, pos_o, cnt_o):
    t = lg_ref.shape[0]
    lg = lg_ref[...]
    lane_i = lax.broadcasted_iota(jnp.int32, (1, LANE), 1)
    lane = lane_i.astype(F32)
    big = float(LANE)
    neg = -jnp.inf
    is_g = lane_i < N_GROUPS
    lgm = jnp.where(is_g, lg, neg)
    mg = jnp.max(lgm, axis=-1, keepdims=True)
    p_sel = 1.0 / jnp.sum(jnp.where(is_g, jnp.exp(lgm - mg), 0.0), axis=-1, keepdims=True)
    gidx = jnp.min(jnp.where(jnp.logical_and(is_g, lgm == mg), lane, big), axis=-1, keepdims=True)
    e_lane = lane_i - N_GROUPS
    in_grp = jnp.logical_and(jnp.logical_and(e_lane >= 0, e_lane < N_EXPERTS),
                             jnp.floor((lane - N_GROUPS) * (1.0 / EXP_PER_GROUP)) == gidx)
    le = jnp.where(in_grp, lg, neg)
    v1 = jnp.max(le, axis=-1, keepdims=True)
    i1 = jnp.min(jnp.where(jnp.logical_and(in_grp, le == v1), lane, big), axis=-1, keepdims=True)
    is1 = lane == i1
    le2 = jnp.where(is1, neg, le)
    v2 = jnp.max(le2, axis=-1, keepdims=True)
    rest = jnp.logical_and(in_grp, jnp.logical_not(is1))
    i2 = jnp.min(jnp.where(jnp.logical_and(rest, le2 == v2), lane, big), axis=-1, keepdims=True)
    is2 = lane == i2
    e2 = jnp.exp(v2 - v1)
    w1 = 1.0 / (1.0 + e2)
    w2 = e2 / (1.0 + e2)
    gw = jnp.where(is1, p_sel * w1, jnp.where(is2, p_sel * w2, 0.0))
    sel = jnp.logical_or(jnp.logical_or(is1, is2), lane == gidx)
    self_ = jnp.where(sel, 1.0, 0.0)
    ri = lax.broadcasted_iota(jnp.int32, (ROUTE_BLK, ROUTE_BLK), 0)
    ci = lax.broadcasted_iota(jnp.int32, (ROUTE_BLK, ROUTE_BLK), 1)
    tri = jnp.where(ri > ci, 1.0, 0.0).astype(BF16)
    run = jnp.zeros((1, LANE), F32)
    pos_blocks = []
    for b in range(t // ROUTE_BLK):
        blk = self_[b * ROUTE_BLK:(b + 1) * ROUTE_BLK]
        pos_blocks.append(_dot(tri, blk.astype(BF16)) + run)
        run = run + jnp.sum(blk, axis=0, keepdims=True)
    pos = jnp.where(sel, jnp.concatenate(pos_blocks, axis=0), -1.0)
    gw_o[...] = gw.T
    pos_o[...] = pos.T
    cnt_o[...] = jnp.broadcast_to(run, (8, LANE)).astype(jnp.int32)


def _route(logits):
    ntok = logits.shape[0]
    ntile = ntok // MOE_T
    return pl.pallas_call(
        _route_kernel,
        grid=(ntile,),
        in_specs=[pl.BlockSpec((MOE_T, LANE), lambda t: (t, 0))],
        out_specs=[pl.BlockSpec((None, LANE, MOE_T), lambda t: (t, 0, 0)),
                   pl.BlockSpec((None, LANE, MOE_T), lambda t: (t, 0, 0)),
                   pl.BlockSpec((None, 8, LANE), lambda t: (t, 0, 0))],
        out_shape=[jax.ShapeDtypeStruct((ntile, LANE, MOE_T), F32),
                   jax.ShapeDtypeStruct((ntile, LANE, MOE_T), F32),
                   jax.ShapeDtypeStruct((ntile, 8, LANE), jnp.int32)],
        compiler_params=_params(("parallel",)),
        name="moe_route",
    )(logits)


def _moe_kernel(cnt_ref, h_ref, pos_ref, gw_ref, wg_ref, wu_ref, wd_ref, o_ref, acc_ref):
    tp = pl.program_id(0)
    g = pl.program_id(1)

    @pl.when(g == 0)
    def _():
        acc_ref[...] = jnp.zeros_like(acc_ref)

    for s in range(MOE_SUB):
        rows = slice(s * MOE_T, (s + 1) * MOE_T)
        n = cnt_ref[(tp * MOE_SUB + s) * N_GROUPS + g]
        prow = pos_ref[s, pl.ds(g, 1), :]
        grows = [gw_ref[s, pl.ds(N_GROUPS + EXP_PER_GROUP * g + e, 1), :] for e in range(EXP_PER_GROUP)]

        def chunk(c, carry, rows=rows, prow=prow, grows=grows):
            slot = (lax.broadcasted_iota(jnp.int32, (MOE_CH, 1), 0) + c * MOE_CH).astype(F32)
            hit = prow == slot
            onehot = jnp.where(hit, 1.0, 0.0).astype(BF16)
            hc = _dot(onehot, h_ref[rows, :]).astype(BF16)
            y = jnp.zeros((MOE_CH, acc_ref.shape[1]), F32)
            for e in range(EXP_PER_GROUP):
                gcol = jnp.sum(jnp.where(hit, grows[e], 0.0), axis=-1, keepdims=True)
                hid = _silu(_dot(hc, wg_ref[e])) * _dot(hc, wu_ref[e])
                y = y + _dot((hid * gcol).astype(BF16), wd_ref[e])
            acc_ref[rows, :] += _dot_tn(onehot, y.astype(BF16))
            return carry

        lax.fori_loop(0, (n + MOE_CH - 1) // MOE_CH, chunk, 0)

    @pl.when(g == pl.num_programs(1) - 1)
    def _():
        o_ref[...] = acc_ref[...].astype(o_ref.dtype)


def _moe(counts, h2, pos_t, gw_t, wg, wu, wd):
    ntok, d = h2.shape
    ntile = ntok // MOE_T
    gs = pltpu.PrefetchScalarGridSpec(
        num_scalar_prefetch=1,
        grid=(ntile // MOE_SUB, N_GROUPS),
        in_specs=[pl.BlockSpec((MOE_SUB * MOE_T, d), lambda t, g, c: (t, 0)),
                  pl.BlockSpec((MOE_SUB, LANE, MOE_T), lambda t, g, c: (t, 0, 0)),
                  pl.BlockSpec((MOE_SUB, LANE, MOE_T), lambda t, g, c: (t, 0, 0)),
                  pl.BlockSpec((EXP_PER_GROUP, d, D_EXPERT), lambda t, g, c: (g, 0, 0)),
                  pl.BlockSpec((EXP_PER_GROUP, d, D_EXPERT), lambda t, g, c: (g, 0, 0)),
                  pl.BlockSpec((EXP_PER_GROUP, D_EXPERT, d), lambda t, g, c: (g, 0, 0))],
        out_specs=pl.BlockSpec((MOE_SUB * MOE_T, d), lambda t, g, c: (t, 0)),
        scratch_shapes=[pltpu.VMEM((MOE_SUB * MOE_T, d), F32)],
    )
    return pl.pallas_call(
        _moe_kernel,
        grid_spec=gs,
        out_shape=jax.ShapeDtypeStruct((ntok, d), BF16),
        compiler_params=_params(("parallel", "arbitrary")),
        name="moe_experts",
    )(counts, h2, pos_t, gw_t, wg, wu, wd)


def _final_kernel(xn_ref, ff_ref, g2_ref, g_ref, o_ref):
    x = xn_ref[...] + g2_ref[...] * ff_ref[...]
    y = x * lax.rsqrt(jnp.mean(x * x, axis=-1, keepdims=True) + EPS)
    o_ref[...] = y * g_ref[...]


def _final(xn, ff, mod, g, nb, n_lat, ntot):
    d = xn.shape[-1]
    nl = n_lat // TM
    row = pl.BlockSpec((None, TM, d), lambda b, i: (b, i, 0))
    return pl.pallas_call(
        _final_kernel,
        grid=(nb, nl),
        in_specs=[row, row, _mod_spec_d(5, nb, nl, d), pl.BlockSpec(g.shape, lambda b, i: (0, 0))],
        out_specs=row,
        out_shape=jax.ShapeDtypeStruct((nb, n_lat, d), F32),
        compiler_params=_params(("parallel", "arbitrary")),
        name="final_norm",
    )(xn, ff, mod, g)


def _rope_swap_perm(width, dim):
    nf = dim // 4
    j = jnp.arange(width)
    base = (j // (2 * nf)) * (2 * nf)
    return base + (j % (2 * nf) + nf) % (2 * nf)


def _rope_tables(n_lat, n_ctx, dim, width):
    nf = dim // 4
    t = jnp.arange(n_lat)
    row = (t // GRID_W).astype(F32)
    col = (t % GRID_W).astype(F32)
    inv = ROPE_BASE ** (-jnp.arange(nf, dtype=F32) / nf)
    j = jnp.arange(width) % dim
    axis = j // (2 * nf)
    pos = jnp.where(axis[None, :] == 0, row[:, None], col[:, None])
    ang = pos * inv[j % nf][None, :]
    sign = jnp.where(j % (2 * nf) < nf, -1.0, 1.0).astype(F32)
    cos = jnp.concatenate([jnp.cos(ang), jnp.ones((n_ctx, width), F32)], axis=0)
    sin = jnp.concatenate([jnp.sin(ang) * sign[None, :], jnp.zeros((n_ctx, width), F32)], axis=0)
    return cos, sin


def _block_ones(width, group, value):
    j = jnp.arange(width)
    return jnp.where((j[:, None] // group) == (j[None, :] // group), value, 0.0).astype(BF16)


def _slot_cols():
    rep = WA_HEADS // WA_KV_HEADS
    cols = []
    for s in range(rep):
        cols.append(jnp.arange(WA_DIM) + WA_DIM * s)
        cols.append(jnp.arange(WA_DIM) + WA_DIM * (rep + s))
    return jnp.concatenate(cols)


def _build_w_in(w):
    d = w.shape[0]
    s1, s2 = A_COLS, A_COLS + B_COLS
    wa, wb, wc = w[:, :s1], w[:, s1:s2], w[:, s2:]
    ab = jnp.pad(wa[:, QKV_W + A_W:], ((0, 0), (0, LANE - 4 * DN_HEADS)))
    bq, bk, bv = wb[:, :B_QK_W], wb[:, B_QK_W:2 * B_QK_W], wb[:, 2 * B_QK_W:]
    pb = _rope_swap_perm(B_QK_W, DA_QK)
    cq = wc[:, :C_W][:, _slot_cols()]
    ck, cv = wc[:, C_W:C_W + C_KV_W], wc[:, C_W + C_KV_W:]
    pcq = _rope_swap_perm(C_W, WA_DIM)
    pck = _rope_swap_perm(C_KV_W, WA_DIM)
    cat = jnp.concatenate([wa[:, :QKV_W], wa[:, QKV_W:QKV_W + A_W], ab,
                           bq, bk, bv, bq[:, pb], bk[:, pb],
                           cq, ck, cv, cq[:, pcq], ck[:, pck]], axis=1)
    assert cat.shape == (d, _O_END)
    return cat.astype(BF16)


def _expand_mats():
    r = jnp.arange(LANE)[:, None]
    h = (jnp.arange(A_QK_W) // DN_DK)[None, :]
    eg = jnp.stack([(r == DN_HEADS * d + h) for d in range(2)]).astype(BF16)
    eb = jnp.stack([(r == 2 * DN_HEADS + DN_HEADS * d + h) for d in range(2)]).astype(BF16)
    return eg, eb


def kernel(x, c, ctx, c_ctx, ada_w, ada_b, norm1_g, norm2_g, w_in, dn_conv_w, dn_a_log, dn_dt_bias, dn_norm_g, da_lambda, da_subln_g, wa_sink, w_out, router_group_w, router_group_b, router_expert_w, router_expert_b, exp_w_gate, exp_w_up, exp_w_down, final_norm_g):
    nb, n_lat, d = x.shape
    n_ctx = ctx.shape[1]
    depth = ada_w.shape[0]
    assert n_ctx == TM and n_lat % TM == 0 and n_lat >= 3 * TM
    ntot = n_lat + n_ctx
    ntok = nb * ntot
    assert ntok % (MOE_SUB * MOE_T) == 0

    cc = jnp.zeros((16, d), F32).at[:nb].set(c).at[nb].set(c_ctx)
    cosb, sinb = _rope_tables(n_lat, n_ctx, DA_QK, B_QK_W)
    cosc, sinc = _rope_tables(n_lat, n_ctx, WA_DIM, C_KV_W)
    tabs = (cosb, sinb, cosc, sinc)
    ones_a = _block_ones(A_W, DN_DV, 1.0)
    mean_a = _block_ones(A_W, DN_DV, 1.0 / DN_DV)
    mean_b = _block_ones(B_W, DA_V, 1.0 / DA_V)
    eg, eb = _expand_mats()
    slot_rows = _slot_cols()
    pad_lane = lambda v: jnp.pad(v.reshape(1, -1), ((0, 0), (0, LANE - v.size)))

    mods = [_ada(cc, ada_w[li], ada_b[li]).reshape(16, 6, 1, d) for li in range(depth)]
    xn = ff = None
    for li in range(depth):
        mod = mods[li]
        w_cat = _build_w_in(w_in[li])
        n1g = norm1_g[li].reshape(1, d)
        if li == 0:
            outs = _in_proj(("split", x, ctx), (mod,), n1g, w_cat, tabs, nb, ntot)
        else:
            outs = _in_proj(("res", xn, ff.reshape(nb, ntot, d)), (mod, mods[li - 1]), n1g, w_cat, tabs, nb, ntot)
        xcur, outs = outs[0], outs[1:]
        zqkv, gate, ab, qb, kb, vb, qc, kc, vc = outs

        conv_w8 = jnp.pad(dn_conv_w[li], ((0, 8 - DN_CONV), (0, 0)))
        q, k, v = _dn_prep(zqkv, conv_w8, ones_a)
        of, ob = _dn_scan(q, k, v, ab, pad_lane(dn_a_log[li]), pad_lane(dn_dt_bias[li]), eg, eb)

        lam_init = 0.8 - 0.6 * math.exp(-0.3 * li)
        yb = _diff_attn(qb, kb, vb, da_lambda[li], jnp.tile(da_subln_g[li], DA_HEADS).reshape(B_W, 1),
                        mean_b, lam_init)
        yw = _win_attn(qc, kc, vc, jnp.pad(wa_sink[li], (0, 8 - WA_HEADS)), n_lat)

        wo = w_out[li]
        wa_o = wo[:A_W].astype(BF16)
        wb_o = wo[A_W:A_W + B_W].astype(BF16)
        wc_o = wo[A_W + B_W:][slot_rows].astype(BF16)
        wr = jnp.pad(jnp.concatenate([router_group_w[li], router_expert_w[li]], axis=1),
                     ((0, 0), (0, LANE - N_GROUPS - N_EXPERTS)))
        br = pad_lane(jnp.concatenate([router_group_b[li], router_expert_b[li]]))
        xn, h2, logits = _out_proj(xcur, of, ob, gate, yb, yw, wa_o, wb_o, wc_o,
                                   jnp.tile(dn_norm_g[li], DN_HEADS).reshape(1, A_W), mean_a, mod,
                                   norm2_g[li].reshape(1, d), wr, br, nb, ntot)

        gw_t, pos_t, cnt = _route(logits.reshape(ntok, LANE))
        counts = cnt[:, 0, :N_GROUPS].reshape(-1)
        ff = _moe(counts, h2.reshape(ntok, d), pos_t, gw_t, exp_w_gate[li].astype(BF16),
                  exp_w_up[li].astype(BF16), exp_w_down[li].astype(BF16))

    return _final(xn, ff.reshape(nb, ntot, d), mods[depth - 1], final_norm_g.reshape(1, d), nb, n_lat, ntot)
```

```python
import functools
import math

import jax
import jax.numpy as jnp
from jax import lax
from jax.experimental import pallas as pl
from jax.experimental.pallas import tpu as pltpu

F32 = jnp.float32
BF16 = jnp.bfloat16

GRID_W = 64
EPS = 1e-6
LOG2E = math.log2(math.e)
ROPE_BASE = 10000.0
DN_HEADS = 6
DN_DK = 64
DN_DV = 64
DN_CONV = 5
DN_CHUNK = 64
DA_HEADS = 4
DA_QK = 32
DA_V = 64
WA_HEADS = 6
WA_KV_HEADS = 2
WA_DIM = 64
WINDOW = 128
N_GROUPS = 4
EXP_PER_GROUP = 4
N_EXPERTS = 16
D_EXPERT = 512

A_QK_W = DN_HEADS * DN_DK
A_W = DN_HEADS * DN_DV
QKV_W = 2 * A_QK_W + A_W
B_W = DA_HEADS * DA_V
B_QK_W = 2 * DA_HEADS * DA_QK
C_W = WA_HEADS * WA_DIM
C_KV_W = WA_KV_HEADS * WA_DIM
A_COLS = QKV_W + A_W + 4 * DN_HEADS
B_COLS = 2 * B_QK_W + B_W

LANE = 128
TM = 256
MOE_T = 1024
MOE_SUB = 2
MOE_CH_SHORT = 96
MOE_CH = 288
ROUTE_BLK = 256
DN_EXACT_ROUNDS = 5
DA_KEY_GROUP = 8
VMEM_LIMIT = 56 * 1024 * 1024

_O_QKV = 0
_O_GATE = _O_QKV + QKV_W
_O_AB = _O_GATE + A_W
_O_BQ = _O_AB + LANE
_O_BK = _O_BQ + B_QK_W
_O_BV = _O_BK + B_QK_W
_O_BQS = _O_BV + B_W
_O_BKS = _O_BQS + B_QK_W
_O_CQ = _O_BKS + B_QK_W
_O_CK = _O_CQ + C_W
_O_CV = _O_CK + C_KV_W
_O_CQS = _O_CV + C_KV_W
_O_CKS = _O_CQS + C_W
_O_END = _O_CKS + C_KV_W


def _dot(a, b):
    return jnp.dot(a, b, preferred_element_type=F32)


def _dot_nt(a, b):
    return lax.dot_general(a, b, (((1,), (1,)), ((), ())), preferred_element_type=F32)


def _dot_tn(a, b):
    return lax.dot_general(a, b, (((0,), (0,)), ((), ())), preferred_element_type=F32)


def _split2(a):
    hi = a.astype(BF16)
    lo = (a - hi.astype(F32)).astype(BF16)
    return hi, lo


def _split3(a):
    hi = a.astype(BF16)
    r = a - hi.astype(F32)
    mid = r.astype(BF16)
    lo = (r - mid.astype(F32)).astype(BF16)
    return hi, mid, lo


def _dot3(a, b):
    ah, al = _split2(a)
    bh, bl = _split2(b)
    return _dot(ah, bh) + (_dot(ah, bl) + _dot(al, bh))


def _dot_exact_rhs(a, b_bf16, parts=3):
    sp = _split3(a) if parts == 3 else _split2(a)
    out = _dot(sp[0], b_bf16)
    for p in sp[1:]:
        out = out + _dot(p, b_bf16)
    return out


def _dot_exact_lhs(a_bf16, b, parts=3):
    sp = _split3(b) if parts == 3 else _split2(b)
    out = _dot(a_bf16, sp[0])
    for p in sp[1:]:
        out = out + _dot(a_bf16, p)
    return out


def _col_reduce(x, op, slab=64):
    n, w = x.shape
    if n > slab and n % slab == 0:
        x = op(x.reshape(n // slab, slab, w), axis=0)
    return op(x, axis=0, keepdims=True)


def _silu(x):
    return x * jax.nn.sigmoid(x)


def _softplus(x):
    return jnp.maximum(x, 0.0) + jnp.log1p(jnp.exp(-jnp.abs(x)))


def _params(sem):
    return pltpu.CompilerParams(dimension_semantics=sem, vmem_limit_bytes=VMEM_LIMIT)


def _ada_kernel(c_ref, w_ref, b_ref, o_ref):
    o_ref[...] = _dot3(_silu(c_ref[...]), w_ref[...]) + b_ref[...]


def _ada(cc, w, b):
    rows, d = cc.shape
    n = w.shape[1]
    tn = n // 4
    return pl.pallas_call(
        _ada_kernel,
        grid=(n // tn,),
        in_specs=[pl.BlockSpec((rows, d), lambda j: (0, 0)),
                  pl.BlockSpec((d, tn), lambda j: (0, j)),
                  pl.BlockSpec((1, tn), lambda j: (0, j))],
        out_specs=pl.BlockSpec((rows, tn), lambda j: (0, j)),
        out_shape=jax.ShapeDtypeStruct((rows, n), F32),
        compiler_params=_params(("arbitrary",)),
        name="ada_mod",
    )(cc, w, b.reshape(1, n))


def _mod_spec_d(k, nb, nl, d):
    return pl.BlockSpec((None, None, 1, d), lambda b, i: (jnp.where(i == nl, nb, b), k, 0, 0))


def _rms_mod(x, g, sc, sh):
    y = x * lax.rsqrt(jnp.mean(x * x, axis=-1, keepdims=True) + EPS)
    return (y * g) * (1.0 + sc) + sh


def _in_proj_kernel(fuse_res, *refs):
    if fuse_res:
        xn_ref, ff_ref, g2_ref = refs[:3]
        refs = refs[3:]
    else:
        xl_ref, xc_ref = refs[:2]
        refs = refs[2:]
    (sc_ref, sh_ref, g_ref, w_ref, cosb_ref, sinb_ref, cosc_ref, sinc_ref) = refs[:8]
    outs = refs[8:]
    if fuse_res:
        x = xn_ref[...] + g2_ref[...] * ff_ref[...]
    else:
        is_ctx = pl.program_id(1) == pl.num_programs(1) - 1
        x = jnp.where(is_ctx, xc_ref[...], xl_ref[...])
    outs[0][...] = x
    outs = outs[1:]
    (zqkv_o, gate_o, ab_o, qb_o, kb_o, vb_o, qc_o, kc_o, vc_o) = outs
    hb = _rms_mod(x, g_ref[...], sc_ref[...], sh_ref[...]).astype(BF16)

    def seg(a, b):
        return _dot(hb, w_ref[:, a:b])

    zqkv_o[...] = seg(_O_QKV, _O_GATE)
    gate_o[...] = seg(_O_GATE, _O_AB)
    ab_o[...] = seg(_O_AB, _O_BQ)
    cb = cosb_ref[...]
    sb = sinb_ref[...]
    qb_o[...] = ((seg(_O_BQ, _O_BK) * cb + seg(_O_BQS, _O_BKS) * sb) * (DA_QK ** -0.5 * LOG2E)).astype(BF16)
    kb_o[...] = (seg(_O_BK, _O_BV) * cb + seg(_O_BKS, _O_CQ) * sb).astype(BF16)
    vb_o[...] = seg(_O_BV, _O_BQS).T.astype(BF16)
    cc = cosc_ref[...]
    sc_ = sinc_ref[...]
    cc3 = jnp.concatenate([cc, cc, cc], axis=1)
    sc3 = jnp.concatenate([sc_, sc_, sc_], axis=1)
    qc_o[...] = ((seg(_O_CQ, _O_CK) * cc3 + seg(_O_CQS, _O_CKS) * sc3) * (WA_DIM ** -0.5)).astype(BF16)
    kc_o[...] = (seg(_O_CK, _O_CV) * cc + seg(_O_CKS, _O_END) * sc_).astype(BF16)
    vc_o[...] = seg(_O_CV, _O_CQS).astype(BF16)


def _in_proj(x_parts, mod, norm_g, w_cat, tabs, nb, ntot):
    fuse_res = x_parts[0] == "res"
    x_parts = x_parts[1:]
    d = x_parts[0].shape[-1]
    nt = ntot // TM
    nl = nt - 1
    row = lambda w: pl.BlockSpec((None, TM, w), lambda b, i: (b, i, 0))
    tab = lambda w: pl.BlockSpec((TM, w), lambda b, i: (i, 0))
    const = lambda a: pl.BlockSpec(a.shape, lambda b, i: (0,) * a.ndim)
    if fuse_res:
        (xn, ff), prev_mod = x_parts, mod[1]
        ins = [xn, ff, prev_mod]
        in_specs = [row(d), row(d), _mod_spec_d(5, nb, nl, d)]
        cur_mod = mod[0]
    else:
        ins = list(x_parts)
        in_specs = [pl.BlockSpec((None, TM, d), lambda b, i: (b, jnp.minimum(i, nl - 1), 0)),
                    pl.BlockSpec((None, TM, d), lambda b, i: (b, 0, 0))]
        cur_mod = mod[0]
    ins += [cur_mod, cur_mod, norm_g, w_cat, *tabs]
    in_specs += [_mod_spec_d(1, nb, nl, d), _mod_spec_d(0, nb, nl, d), const(norm_g), const(w_cat),
                 tab(B_QK_W), tab(B_QK_W), tab(C_KV_W), tab(C_KV_W)]
    widths = [(QKV_W, F32), (A_W, F32), (LANE, F32), (B_QK_W, BF16), (B_QK_W, BF16), (None, BF16),
              (C_W, BF16), (C_KV_W, BF16), (C_KV_W, BF16)]
    widths = [(d, F32)] + widths
    return pl.pallas_call(
        functools.partial(_in_proj_kernel, fuse_res),
        grid=(nb, nt),
        in_specs=in_specs,
        out_specs=[pl.BlockSpec((None, None, B_W, TM), lambda b, i: (b, i, 0, 0)) if w is None else row(w)
                   for w, _ in widths],
        out_shape=[jax.ShapeDtypeStruct((nb, nt, B_W, TM) if w is None else (nb, ntot, w), dt)
                   for w, dt in widths],
        compiler_params=_params(("parallel", "arbitrary")),
        name="in_proj",
    )(*ins)


def _dn_prep_kernel(zc_ref, zp_ref, zn_ref, w_ref, ones_ref, q_o, k_o, v_o, ext_ref):
    i = pl.program_id(1)
    nl = pl.num_programs(1) - 1
    prev_ok = jnp.logical_and(i >= 1, i <= nl - 1)
    next_ok = i <= nl - 2
    ext_ref[0:8, :] = jnp.where(prev_ok, zp_ref[...], 0.0)
    ext_ref[8:8 + TM, :] = zc_ref[...]
    ext_ref[8 + TM:16 + TM, :] = jnp.where(next_ok, zn_ref[...], 0.0)
    half = DN_CONV // 2
    acc = w_ref[0:1, :] * ext_ref[8 - half:8 - half + TM, :]
    for j in range(1, DN_CONV):
        acc = acc + w_ref[j:j + 1, :] * ext_ref[8 - half + j:8 - half + j + TM, :]
    y = _silu(acc)
    ones = ones_ref[...]

    def l2n(t):
        ss = _dot_exact_rhs(t * t, ones, parts=2)
        return t * lax.rsqrt(ss + EPS)

    q_o[...] = l2n(y[:, :A_QK_W]) * (DN_DK ** -0.5)
    k_o[...] = l2n(y[:, A_QK_W:2 * A_QK_W])
    v_o[...] = y[:, 2 * A_QK_W:]


def _dn_prep(zqkv, conv_w8, ones_a):
    nb, ntot, w = zqkv.shape
    nt = ntot // TM
    r8 = TM // 8
    row = lambda ww: pl.BlockSpec((None, TM, ww), lambda b, i: (b, i, 0))
    return pl.pallas_call(
        _dn_prep_kernel,
        grid=(nb, nt),
        in_specs=[row(w),
                  pl.BlockSpec((None, 8, w), lambda b, i: (b, jnp.maximum(i * r8 - 1, 0), 0)),
                  pl.BlockSpec((None, 8, w), lambda b, i: (b, jnp.minimum(i * r8 + r8, ntot // 8 - 1), 0)),
                  pl.BlockSpec(conv_w8.shape, lambda b, i: (0, 0)),
                  pl.BlockSpec(ones_a.shape, lambda b, i: (0, 0))],
        out_specs=[row(A_QK_W), row(A_QK_W), row(A_W)],
        out_shape=[jax.ShapeDtypeStruct((nb, ntot, A_QK_W), F32)] * 3,
        scratch_shapes=[pltpu.VMEM((TM + 16, w), F32)],
        compiler_params=_params(("parallel", "arbitrary")),
        name="dn_prep",
    )(zqkv, zqkv, zqkv, conv_w8, ones_a)


def _dn_pre(d, rows, q_ref, k_ref, v_ref, ab_ref, alog, dtb, eg_ref, eb_ref):
    c = DN_CHUNK
    q = q_ref[rows, :]
    k = k_ref[rows, :]
    v = v_ref[rows, :]
    ab = ab_ref[rows, :]
    g = -jnp.exp(alog) * _softplus(ab + dtb)
    beta = jax.nn.sigmoid(ab)
    ri = lax.broadcasted_iota(jnp.int32, (c, 2 * c), 0)
    ci = lax.broadcasted_iota(jnp.int32, (c, 2 * c), 1) % c
    if d == 0:
        incl, strict = ri >= ci, ri > ci
    else:
        incl, strict = ri <= ci, ri < ci
    cum = jnp.where(incl[:, :c], 1.0, 0.0).astype(BF16)
    gc = _dot_exact_lhs(cum, g)
    last = c - 1 if d == 0 else 0
    g_last = gc[last:last + 1, :]
    egc = jnp.exp(gc)
    ekd = jnp.exp(g_last - gc)
    egl = jnp.broadcast_to(jnp.exp(g_last), (8, LANE))
    eg = eg_ref[d]
    eb = eb_ref[d]
    beta_x = _dot_exact_rhs(beta, eb)
    gx = _dot_exact_rhs(jnp.concatenate([egc, ekd, gc, egl], axis=0), eg)
    egc_x, ekd_x, gc_x, egl_x = gx[0:c], gx[c:2 * c], gx[2 * c:3 * c], gx[3 * c:3 * c + 1]
    gc_t = jnp.concatenate([gc, gc], axis=0).T
    kbeta = k * beta_x
    vbeta = v * beta_x
    wrhs = kbeta * egc_x
    qd = q * egc_x
    kd = k * ekd_x
    lane = lax.broadcasted_iota(jnp.int32, (1, LANE), 1)
    lo = lane < DN_DK
    pairs = []
    for p in range(DN_HEADS // 2):
        sl = slice(LANE * p, LANE * p + LANE)
        k_s, q_s = k[:, sl], q[:, sl]
        kb_s, vb_s, wr_s = kbeta[:, sl], vbeta[:, sl], wrhs[:, sl]
        gcx_s = gc_x[:, sl]
        k_sb = k_s.astype(BF16)
        k_sb2 = jnp.concatenate([k_sb, k_sb], axis=0)
        mats, rhss, attns = [], [], []
        for j in range(2):
            h = 2 * p + j
            mine = lo if j == 0 else jnp.logical_not(lo)
            gcx_r = pltpu.roll(gcx_s, DN_DK, 1)
            gcol = jnp.where(lo, gcx_s, gcx_r) if j == 0 else jnp.where(lo, gcx_r, gcx_s)
            grow = gc_t[DN_HEADS * d + h:DN_HEADS * d + h + 1, :]
            diff = gcol - grow
            dec = jnp.where(incl, jnp.exp(jnp.where(incl, diff, 0.0)), 0.0)
            kk = _dot_nt(jnp.where(mine, kb_s, 0.0).astype(BF16), k_sb2)
            mats.append(jnp.where(strict, kk * dec, 0.0))
            qk = _dot_nt(jnp.where(mine, q_s, 0.0).astype(BF16), k_sb)
            attns.append((qk * dec[:, :c]).astype(BF16))
            if j == 0:
                rhss.append(jnp.where(lo, vb_s, pltpu.roll(wr_s, DN_DK, 1)))
            else:
                rhss.append(jnp.where(lo, pltpu.roll(vb_s, DN_DK, 1), wr_s))
        pairs.append(dict(mats=mats, rhss=rhss, attns=attns, qd=qd[:, sl].astype(BF16),
                          kd_t=kd[:, sl].T.astype(BF16), egl=egl_x[:, sl]))
    return pairs


def _dn_solve(mats, rhss):
    c = DN_CHUNK
    lo = lax.broadcasted_iota(jnp.int32, (1, 2 * c), 1) < c
    pw, xs = list(mats), list(rhss)
    rounds = 6
    for r in range(rounds):
        for n in range(len(pw)):
            last = r == rounds - 1
            ph = pw[n].astype(BF16)
            xh = xs[n].astype(BF16)
            bh = xh if last else jnp.concatenate([xh, ph], axis=1)
            if r < DN_EXACT_ROUNDS:
                plo = (pw[n] - ph.astype(F32)).astype(BF16)
                xl = (xs[n] - xh.astype(F32)).astype(BF16)
                bl = xl if last else jnp.concatenate([xl, plo], axis=1)
                lhs = jnp.concatenate([jnp.where(lo, ph, plo), ph[:, :c]], axis=1)
                both = _dot(lhs, jnp.concatenate([bh, bh, bl], axis=0))
            else:
                both = _dot(ph[:, :c], bh)
            px = both[:, :2 * c]
            if r < rounds - 1:
                pw[n] = both[:, 2 * c:]
            xs[n] = xs[n] - px if r == 0 else xs[n] + px
    return xs


def _dn_scan_kernel(alog_ref, dtb_ref, eg_ref, eb_ref,
                    qf, kf, vf, abf, qb, kb, vb, abb, of_ref, ob_ref, s_ref):
    i = pl.program_id(1)

    @pl.when(i == 0)
    def _():
        s_ref[...] = jnp.zeros_like(s_ref)

    alog = alog_ref[...]
    dtb = dtb_ref[...]
    nch = TM // DN_CHUNK
    npair = DN_HEADS // 2
    lane = lax.broadcasted_iota(jnp.int32, (1, LANE), 1)
    lo = lane < DN_DK
    ri2 = lax.broadcasted_iota(jnp.int32, (LANE, LANE), 0)
    ci2 = lax.broadcasted_iota(jnp.int32, (LANE, LANE), 1)
    bdiag = (ri2 < DN_DK) == (ci2 < DN_DK)

    pairs, rows_of = [], []
    for g in range(nch):
        rf = slice(g * DN_CHUNK, (g + 1) * DN_CHUNK)
        rb = slice((nch - 1 - g) * DN_CHUNK, (nch - g) * DN_CHUNK)
        pairs += (_dn_pre(0, rf, qf, kf, vf, abf, alog, dtb, eg_ref, eb_ref)
                  + _dn_pre(1, rb, qb, kb, vb, abb, alog, dtb, eg_ref, eb_ref))
        rows_of += [rf] * npair + [rb] * npair
    xs = _dn_solve([m for pr in pairs for m in pr["mats"]], [r for pr in pairs for r in pr["rhss"]])
    for m, pr in enumerate(pairs):
        n = m % (2 * npair)
        d, p = divmod(n, npair)
        x0, x1 = xs[2 * m], xs[2 * m + 1]
        u = jnp.where(lo, x0, pltpu.roll(x1, DN_DK, 1))
        w = jnp.where(lo, pltpu.roll(x0, DN_DK, 1), x1)
        s = s_ref[n]
        sb = s.astype(BF16)
        v_new = u - _dot(w.astype(BF16), sb)
        vn_b = v_new.astype(BF16)
        o = _dot(pr["qd"], sb)
        o = o + _dot(pr["attns"][0], jnp.where(lo, vn_b, jnp.zeros_like(vn_b)))
        o = o + _dot(pr["attns"][1], jnp.where(lo, jnp.zeros_like(vn_b), vn_b))
        upd = _dot(pr["kd_t"], vn_b)
        s_ref[n] = s * pr["egl"] + jnp.where(bdiag, upd, 0.0)
        o_ref = of_ref if d == 0 else ob_ref
        o_ref[rows_of[m], LANE * p:LANE * p + LANE] = o


def _dn_scan(q, k, v, ab, alog, dtb, eg, eb):
    nb, ntot, w = q.shape
    nt = ntot // TM
    nl = nt - 1
    fwd = lambda b, i: (b, jnp.where(i == 0, nl, i - 1), 0)
    bwd = lambda b, i: (b, jnp.where(i == 0, nl, nl - i), 0)
    const = lambda a: pl.BlockSpec(a.shape, lambda b, i: (0,) * a.ndim)
    blk = lambda ww, im: pl.BlockSpec((None, TM, ww), im)
    return pl.pallas_call(
        _dn_scan_kernel,
        grid=(nb, nt),
        in_specs=[const(alog), const(dtb), const(eg), const(eb),
                  blk(w, fwd), blk(w, fwd), blk(w, fwd), blk(LANE, fwd),
                  blk(w, bwd), blk(w, bwd), blk(w, bwd), blk(LANE, bwd)],
        out_specs=[blk(w, fwd), blk(w, bwd)],
        out_shape=[jax.ShapeDtypeStruct((nb, ntot, w), F32)] * 2,
        scratch_shapes=[pltpu.VMEM((2 * (DN_HEADS // 2), LANE, LANE), F32)],
        compiler_params=_params(("parallel", "arbitrary")),
        name="dn_scan",
    )(alog, dtb, eg, eb, q, k, v, ab, q, k, v, ab)


def _diff_attn_kernel(lam_init, q_ref, k_ref, vt_ref, lam_ref, g_ref, ones_ref, o_ref, st_ref):
    i = pl.program_id(1)
    nl = pl.num_programs(1) - 1
    lv = lam_ref[...]
    lam = (jnp.exp(jnp.sum(lv[0:1] * lv[1:2], axis=-1, keepdims=True))
           - jnp.exp(jnp.sum(lv[2:3] * lv[3:4], axis=-1, keepdims=True)) + lam_init)
    lane = lax.broadcasted_iota(jnp.int32, (1, B_QK_W), 1)

    nsm = 2 * DA_HEADS

    def run(groups):
        q = q_ref[...]
        m_prev = None
        res = []
        for n in range(nsm + 1):
            if n < nsm:
                slot = slice(DA_QK * n // LANE * LANE, DA_QK * n // LANE * LANE + LANE)
                lo = DA_QK * n
                qm = jnp.where(jnp.logical_and(lane >= lo, lane < lo + DA_QK), q, jnp.zeros_like(q))[:, slot]
            hp = (n - 1) // 2
            m8 = jnp.full((8, TM), -jnp.inf, F32)
            acc = jnp.zeros((DA_V + 16, TM), F32)
            for grp in groups:
                rows = slice(grp[0] * TM, (grp[-1] + 1) * TM)
                nk = len(grp) * TM
                if n < nsm:
                    st = _dot_nt(k_ref[rows, slot], qm)
                    st_ref[n % 2, rows, :] = st
                    m8 = jnp.maximum(m8, jnp.max(st.reshape(nk // 8, 8, TM), axis=0))
                if n > 0:
                    e = jnp.exp2(st_ref[(n - 1) % 2, rows, :] - m_prev).astype(BF16)
                    vt = jnp.concatenate([vt_ref[c, DA_V * hp:DA_V * hp + DA_V, :] for c in grp], axis=1)
                    lhs = jnp.concatenate([vt, jnp.ones((16, nk), BF16)], axis=0)
                    acc = acc + _dot(lhs, e)
            if n > 0:
                res.append(acc[:DA_V] / acc[DA_V:DA_V + 1])
            if n < nsm:
                m_prev = jnp.max(m8, axis=0, keepdims=True)
        ot = jnp.concatenate([res[2 * h] - lam * res[2 * h + 1] for h in range(DA_HEADS)], axis=0)
        ms = _dot_exact_lhs(ones_ref[...], ot * ot, parts=2)
        yt = (ot * lax.rsqrt(ms + EPS)) * g_ref[...]
        o_ref[...] = (yt * (1.0 - lam_init)).T.astype(BF16)

    ntiles = st_ref.shape[1] // TM

    @pl.when(i < nl)
    def _():
        run([tuple(range(c, min(c + DA_KEY_GROUP, ntiles))) for c in range(0, ntiles, DA_KEY_GROUP)])

    @pl.when(i == nl)
    def _():
        run([(ntiles - 1,)])


def _diff_attn(q, k, vt, lam_vecs, subln_g, ones_b, lam_init):
    nb, ntot, w = q.shape
    nt = ntot // TM
    row = pl.BlockSpec((None, TM, w), lambda b, i: (b, i, 0))
    full = pl.BlockSpec((None, ntot, w), lambda b, i: (b, 0, 0))
    full_t = pl.BlockSpec((None,) + vt.shape[1:], lambda b, i: (b, 0, 0, 0))
    const = lambda a: pl.BlockSpec(a.shape, lambda b, i: (0,) * a.ndim)
    return pl.pallas_call(
        functools.partial(_diff_attn_kernel, lam_init),
        grid=(nb, nt),
        in_specs=[row, full, full_t, const(lam_vecs), const(subln_g), const(ones_b)],
        out_specs=row,
        out_shape=jax.ShapeDtypeStruct((nb, ntot, w), BF16),
        scratch_shapes=[pltpu.VMEM((2, ntot, TM), F32)],
        compiler_params=_params(("parallel", "arbitrary")),
        name="diff_attn",
    )(q, k, vt, lam_vecs, subln_g, ones_b)


def _win_attn_kernel(n_lat, sink_ref, q_ref, k_ref, v_ref, o_ref):
    i = pl.program_id(1)
    nl = pl.num_programs(1) - 1
    rep = WA_HEADS // WA_KV_HEADS
    lane = lax.broadcasted_iota(jnp.int32, (1, LANE), 1)
    lo = lane < WA_DIM
    rowg = lax.broadcasted_iota(jnp.int32, (rep * TM, 1), 0) // TM

    def run(k_all, v_all, bias):
        outs = []
        for g in range(WA_KV_HEADS):
            mine = lo if g == 0 else jnp.logical_not(lo)
            q3 = jnp.concatenate(
                [jnp.where(mine, q_ref[:, LANE * s:LANE * s + LANE], jnp.zeros((TM, LANE), BF16))
                 for s in range(rep)], axis=0)
            s = _dot_nt(q3, k_all)
            if bias is not None:
                s = s + bias
            sk = jnp.zeros((rep * TM, 1), F32)
            for r in range(rep):
                sk = jnp.where(rowg == r, sink_ref[rep * g + r], sk)
            m = jnp.maximum(jnp.max(s, axis=-1, keepdims=True), sk)
            e = jnp.exp(s - m)
            den = jnp.sum(e, axis=-1, keepdims=True) + jnp.exp(sk - m)
            outs.append(_dot(e.astype(BF16), v_all) / den)
        for s in range(rep):
            o_ref[:, LANE * s:LANE * s + LANE] = jnp.where(
                lo, outs[0][TM * s:TM * s + TM], outs[1][TM * s:TM * s + TM]).astype(BF16)

    kc = k_ref[n_lat:, :]
    vc = v_ref[n_lat:, :]
    band = TM + 2 * WINDOW

    @pl.when(i < nl)
    def _():
        start = pl.multiple_of(jnp.clip(i * TM - WINDOW, 0, n_lat - band), WINDOW)
        kb = k_ref[pl.ds(start, band), :]
        vb = v_ref[pl.ds(start, band), :]
        qpos = i * TM + lax.broadcasted_iota(jnp.int32, (TM, 1), 0)
        kpos = start + lax.broadcasted_iota(jnp.int32, (1, band), 1)
        near = jnp.where(jnp.abs(qpos - kpos) <= WINDOW, 0.0, -1e30)
        bias = jnp.concatenate([near, jnp.zeros((TM, kc.shape[0]), F32)], axis=1)
        bias = jnp.concatenate([bias] * rep, axis=0)
        run(jnp.concatenate([kb, kc], axis=0), jnp.concatenate([vb, vc], axis=0), bias)

    @pl.when(i == nl)
    def _():
        run(kc, vc, None)


def _win_attn(q, k, v, sink, n_lat):
    nb, ntot, w = q.shape
    nt = ntot // TM
    kw = k.shape[-1]
    row = pl.BlockSpec((None, TM, w), lambda b, i: (b, i, 0))
    full = pl.BlockSpec((None, ntot, kw), lambda b, i: (b, 0, 0))
    return pl.pallas_call(
        functools.partial(_win_attn_kernel, n_lat),
        grid=(nb, nt),
        in_specs=[pl.BlockSpec(memory_space=pltpu.SMEM), row, full, full],
        out_specs=row,
        out_shape=jax.ShapeDtypeStruct((nb, ntot, w), BF16),
        compiler_params=_params(("parallel", "arbitrary")),
        name="win_attn",
    )(sink, q, k, v)


def _out_proj_kernel(x_ref, of_ref, ob_ref, gate_ref, yb_ref, yw_ref, wa_ref, wb_ref, wc_ref, dng_ref,
                     ones_ref, g1_ref, sc2_ref, sh2_ref, n2_ref, wr_ref, br_ref, xn_o, h2_o, lg_o):
    o = of_ref[...] + ob_ref[...]
    ms = _dot_exact_rhs(o * o, ones_ref[...], parts=2)
    ya = ((o * lax.rsqrt(ms + EPS)) * dng_ref[...]) * _silu(gate_ref[...])
    y = _dot(ya.astype(BF16), wa_ref[...]) + _dot(yb_ref[...], wb_ref[...]) + _dot(yw_ref[...], wc_ref[...])
    xn = x_ref[...] + g1_ref[...] * y
    xn_o[...] = xn
    h2 = _rms_mod(xn, n2_ref[...], sc2_ref[...], sh2_ref[...])
    h2_o[...] = h2.astype(BF16)
    lg_o[...] = _dot3(h2, wr_ref[...]) + br_ref[...]


def _out_proj(x, of, ob, gate, yb, yw, wa, wb, wc, dng, ones_a, mod, n2g, wr, br, nb, ntot):
    d = x.shape[-1]
    nt = ntot // TM
    nl = nt - 1
    row = lambda w: pl.BlockSpec((None, TM, w), lambda b, i: (b, i, 0))
    const = lambda a: pl.BlockSpec(a.shape, lambda b, i: (0,) * a.ndim)
    return pl.pallas_call(
        _out_proj_kernel,
        grid=(nb, nt),
        in_specs=[row(d), row(A_W), row(A_W), row(A_W), row(B_W), row(C_W),
                  const(wa), const(wb), const(wc), const(dng), const(ones_a),
                  _mod_spec_d(2, nb, nl, d), _mod_spec_d(4, nb, nl, d), _mod_spec_d(3, nb, nl, d),
                  const(n2g), const(wr), const(br)],
        out_specs=[row(d), row(d), row(LANE)],
        out_shape=[jax.ShapeDtypeStruct((nb, ntot, d), F32), jax.ShapeDtypeStruct((nb, ntot, d), BF16),
                   jax.ShapeDtypeStruct((nb, ntot, LANE), F32)],
        compiler_params=_params(("parallel", "arbitrary")),
        name="out_proj",
    )(x, of, ob, gate, yb, yw, wa, wb, wc, dng, ones_a, mod, mod, mod, n2g, wr, br)


def _route_kernel(lg_ref, gw_o, pos_o, cnt_o):
    t = lg_ref.shape[0]
    lg = lg_ref[...]
    lane_i = lax.broadcasted_iota(jnp.int32, (1, LANE), 1)
    lane = lane_i.astype(F32)
    big = float(LANE)
    neg = -jnp.inf
    is_g = lane_i < N_GROUPS
    lgm = jnp.where(is_g, lg, neg)
    mg = jnp.max(lgm, axis=-1, keepdims=True)
    p_sel = 1.0 / jnp.sum(jnp.where(is_g, jnp.exp(lgm - mg), 0.0), axis=-1, keepdims=True)
    gidx = jnp.min(jnp.where(jnp.logical_and(is_g, lgm == mg), lane, big), axis=-1, keepdims=True)
    e_lane = lane_i - N_GROUPS
    in_grp = jnp.logical_and(jnp.logical_and(e_lane >= 0, e_lane < N_EXPERTS),
                             jnp.floor((lane - N_GROUPS) * (1.0 / EXP_PER_GROUP)) == gidx)
    le = jnp.where(in_grp, lg, neg)
    v1 = jnp.max(le, axis=-1, keepdims=True)
    i1 = jnp.min(jnp.where(jnp.logical_and(in_grp, le == v1), lane, big), axis=-1, keepdims=True)
    is1 = lane == i1
    le2 = jnp.where(is1, neg, le)
    v2 = jnp.max(le2, axis=-1, keepdims=True)
    rest = jnp.logical_and(in_grp, jnp.logical_not(is1))
    i2 = jnp.min(jnp.where(jnp.logical_and(rest, le2 == v2), lane, big), axis=-1, keepdims=True)
    is2 = lane == i2
    e2 = jnp.exp(v2 - v1)
    w1 = 1.0 / (1.0 + e2)
    w2 = e2 / (1.0 + e2)
    gw = jnp.where(is1, p_sel * w1, jnp.where(is2, p_sel * w2, 0.0))
    sel = jnp.logical_or(jnp.logical_or(is1, is2), lane == gidx)
    self_ = jnp.where(sel, 1.0, 0.0)
    ri = lax.broadcasted_iota(jnp.int32, (ROUTE_BLK, ROUTE_BLK), 0)
    ci = lax.broadcasted_iota(jnp.int32, (ROUTE_BLK, ROUTE_BLK), 1)
    tri = jnp.where(ri > ci, 1.0, 0.0).astype(BF16)
    run = jnp.zeros((1, LANE), F32)
    pos_blocks = []
    for b in range(t // ROUTE_BLK):
        blk = self_[b * ROUTE_BLK:(b + 1) * ROUTE_BLK]
        pos_blocks.append(_dot(tri, blk.astype(BF16)) + run)
        run = run + jnp.sum(blk, axis=0, keepdims=True)
    pos = jnp.where(sel, jnp.concatenate(pos_blocks, axis=0), -1.0)
    gw_o[...] = gw.T
    pos_o[...] = pos.T
    cnt_o[...] = jnp.broadcast_to(run, (8, LANE)).astype(jnp.int32)


def _route(logits):
    ntok = logits.shape[0]
    ntile = ntok // MOE_T
    return pl.pallas_call(
        _route_kernel,
        grid=(ntile,),
        in_specs=[pl.BlockSpec((MOE_T, LANE), lambda t: (t, 0))],
        out_specs=[pl.BlockSpec((None, LANE, MOE_T), lambda t: (t, 0, 0)),
                   pl.BlockSpec((None, LANE, MOE_T), lambda t: (t, 0, 0)),
                   pl.BlockSpec((None, 8, LANE), lambda t: (t, 0, 0))],
        out_shape=[jax.ShapeDtypeStruct((ntile, LANE, MOE_T), F32),
                   jax.ShapeDtypeStruct((ntile, LANE, MOE_T), F32),
                   jax.ShapeDtypeStruct((ntile, 8, LANE), jnp.int32)],
        compiler_params=_params(("parallel",)),
        name="moe_route",
    )(logits)


def _moe_kernel(cnt_ref, h_ref, pos_ref, gw_ref, wg_ref, wu_ref, wd_ref, o_ref, acc_ref):
    tp = pl.program_id(0)
    g = pl.program_id(1)

    @pl.when(g == 0)
    def _():
        acc_ref[...] = jnp.zeros_like(acc_ref)

    for s in range(MOE_SUB):
        rows = slice(s * MOE_T, (s + 1) * MOE_T)
        n = cnt_ref[(tp * MOE_SUB + s) * N_GROUPS + g]
        prow = pos_ref[s, pl.ds(g, 1), :]
        grows = [gw_ref[s, pl.ds(N_GROUPS + EXP_PER_GROUP * g + e, 1), :] for e in range(EXP_PER_GROUP)]

        def chunk(base, ch, rows=rows, prow=prow, grows=grows):
            slot = (lax.broadcasted_iota(jnp.int32, (ch, 1), 0) + base).astype(F32)
            hit = prow == slot
            onehot = jnp.where(hit, 1.0, 0.0).astype(BF16)
            hc = _dot(onehot, h_ref[rows, :]).astype(BF16)
            y = jnp.zeros((ch, acc_ref.shape[1]), F32)
            for e in range(EXP_PER_GROUP):
                gcol = jnp.sum(jnp.where(hit, grows[e], 0.0), axis=-1, keepdims=True)
                hid = _silu(_dot(hc, wg_ref[e])) * _dot(hc, wu_ref[e])
                y = y + _dot((hid * gcol).astype(BF16), wd_ref[e])
            acc_ref[rows, :] += _dot_tn(onehot, y.astype(BF16))

        nfull = n // MOE_CH
        rem = n - nfull * MOE_CH
        nbig = nfull + jnp.where(rem > MOE_CH_SHORT, 1, 0)
        nshort = jnp.where(jnp.logical_and(rem > 0, rem <= MOE_CH_SHORT), 1, 0)

        def big(c, carry, chunk=chunk):
            chunk(c * MOE_CH, MOE_CH)
            return carry

        def short(c, carry, chunk=chunk, nfull=nfull):
            chunk(nfull * MOE_CH, MOE_CH_SHORT)
            return carry

        lax.fori_loop(0, nbig, big, 0)
        lax.fori_loop(0, nshort, short, 0)

    @pl.when(g == pl.num_programs(1) - 1)
    def _():
        o_ref[...] = acc_ref[...].astype(o_ref.dtype)


def _moe(counts, h2, pos_t, gw_t, wg, wu, wd):
    ntok, d = h2.shape
    ntile = ntok // MOE_T
    gs = pltpu.PrefetchScalarGridSpec(
        num_scalar_prefetch=1,
        grid=(ntile // MOE_SUB, N_GROUPS),
        in_specs=[pl.BlockSpec((MOE_SUB * MOE_T, d), lambda t, g, c: (t, 0)),
                  pl.BlockSpec((MOE_SUB, LANE, MOE_T), lambda t, g, c: (t, 0, 0)),
                  pl.BlockSpec((MOE_SUB, LANE, MOE_T), lambda t, g, c: (t, 0, 0)),
                  pl.BlockSpec((EXP_PER_GROUP, d, D_EXPERT), lambda t, g, c: (g, 0, 0)),
                  pl.BlockSpec((EXP_PER_GROUP, d, D_EXPERT), lambda t, g, c: (g, 0, 0)),
                  pl.BlockSpec((EXP_PER_GROUP, D_EXPERT, d), lambda t, g, c: (g, 0, 0))],
        out_specs=pl.BlockSpec((MOE_SUB * MOE_T, d), lambda t, g, c: (t, 0)),
        scratch_shapes=[pltpu.VMEM((MOE_SUB * MOE_T, d), F32)],
    )
    return pl.pallas_call(
        _moe_kernel,
        grid_spec=gs,
        out_shape=jax.ShapeDtypeStruct((ntok, d), BF16),
        compiler_params=_params(("parallel", "arbitrary")),
        name="moe_experts",
    )(counts, h2, pos_t, gw_t, wg, wu, wd)


def _final_kernel(xn_ref, ff_ref, g2_ref, g_ref, o_ref):
    x = xn_ref[...] + g2_ref[...] * ff_ref[...]
    y = x * lax.rsqrt(jnp.mean(x * x, axis=-1, keepdims=True) + EPS)
    o_ref[...] = y * g_ref[...]


def _final(xn, ff, mod, g, nb, n_lat, ntot):
    d = xn.shape[-1]
    nl = n_lat // TM
    row = pl.BlockSpec((None, TM, d), lambda b, i: (b, i, 0))
    return pl.pallas_call(
        _final_kernel,
        grid=(nb, nl),
        in_specs=[row, row, _mod_spec_d(5, nb, nl, d), pl.BlockSpec(g.shape, lambda b, i: (0, 0))],
        out_specs=row,
        out_shape=jax.ShapeDtypeStruct((nb, n_lat, d), F32),
        compiler_params=_params(("parallel", "arbitrary")),
        name="final_norm",
    )(xn, ff, mod, g)


def _rope_swap_perm(width, dim):
    nf = dim // 4
    j = jnp.arange(width)
    base = (j // (2 * nf)) * (2 * nf)
    return base + (j % (2 * nf) + nf) % (2 * nf)


def _rope_tables(n_lat, n_ctx, dim, width):
    nf = dim // 4
    t = jnp.arange(n_lat)
    row = (t // GRID_W).astype(F32)
    col = (t % GRID_W).astype(F32)
    inv = ROPE_BASE ** (-jnp.arange(nf, dtype=F32) / nf)
    j = jnp.arange(width) % dim
    axis = j // (2 * nf)
    pos = jnp.where(axis[None, :] == 0, row[:, None], col[:, None])
    ang = pos * inv[j % nf][None, :]
    sign = jnp.where(j % (2 * nf) < nf, -1.0, 1.0).astype(F32)
    cos = jnp.concatenate([jnp.cos(ang), jnp.ones((n_ctx, width), F32)], axis=0)
    sin = jnp.concatenate([jnp.sin(ang) * sign[None, :], jnp.zeros((n_ctx, width), F32)], axis=0)
    return cos, sin


def _block_ones(width, group, value):
    j = jnp.arange(width)
    return jnp.where((j[:, None] // group) == (j[None, :] // group), value, 0.0).astype(BF16)


def _slot_cols():
    rep = WA_HEADS // WA_KV_HEADS
    cols = []
    for s in range(rep):
        cols.append(jnp.arange(WA_DIM) + WA_DIM * s)
        cols.append(jnp.arange(WA_DIM) + WA_DIM * (rep + s))
    return jnp.concatenate(cols)


def _build_w_in(w):
    d = w.shape[0]
    s1, s2 = A_COLS, A_COLS + B_COLS
    wa, wb, wc = w[:, :s1], w[:, s1:s2], w[:, s2:]
    ab = jnp.pad(wa[:, QKV_W + A_W:], ((0, 0), (0, LANE - 4 * DN_HEADS)))
    bq, bk, bv = wb[:, :B_QK_W], wb[:, B_QK_W:2 * B_QK_W], wb[:, 2 * B_QK_W:]
    pb = _rope_swap_perm(B_QK_W, DA_QK)
    cq = wc[:, :C_W][:, _slot_cols()]
    ck, cv = wc[:, C_W:C_W + C_KV_W], wc[:, C_W + C_KV_W:]
    pcq = _rope_swap_perm(C_W, WA_DIM)
    pck = _rope_swap_perm(C_KV_W, WA_DIM)
    cat = jnp.concatenate([wa[:, :QKV_W], wa[:, QKV_W:QKV_W + A_W], ab,
                           bq, bk, bv, bq[:, pb], bk[:, pb],
                           cq, ck, cv, cq[:, pcq], ck[:, pck]], axis=1)
    assert cat.shape == (d, _O_END)
    return cat.astype(BF16)


def _expand_mats():
    r = jnp.arange(LANE)[:, None]
    h = (jnp.arange(A_QK_W) // DN_DK)[None, :]
    eg = jnp.stack([(r == DN_HEADS * d + h) for d in range(2)]).astype(BF16)
    eb = jnp.stack([(r == 2 * DN_HEADS + DN_HEADS * d + h) for d in range(2)]).astype(BF16)
    return eg, eb


def kernel(x, c, ctx, c_ctx, ada_w, ada_b, norm1_g, norm2_g, w_in, dn_conv_w, dn_a_log, dn_dt_bias, dn_norm_g, da_lambda, da_subln_g, wa_sink, w_out, router_group_w, router_group_b, router_expert_w, router_expert_b, exp_w_gate, exp_w_up, exp_w_down, final_norm_g):
    nb, n_lat, d = x.shape
    n_ctx = ctx.shape[1]
    depth = ada_w.shape[0]
    assert n_ctx == TM and n_lat % TM == 0 and n_lat >= 3 * TM
    ntot = n_lat + n_ctx
    ntok = nb * ntot
    assert ntok % (MOE_SUB * MOE_T) == 0

    cc = jnp.zeros((16, d), F32).at[:nb].set(c).at[nb].set(c_ctx)
    cosb, sinb = _rope_tables(n_lat, n_ctx, DA_QK, B_QK_W)
    cosc, sinc = _rope_tables(n_lat, n_ctx, WA_DIM, C_KV_W)
    tabs = (cosb, sinb, cosc, sinc)
    ones_a = _block_ones(A_W, DN_DV, 1.0)
    mean_a = _block_ones(A_W, DN_DV, 1.0 / DN_DV)
    mean_b = _block_ones(B_W, DA_V, 1.0 / DA_V)
    eg, eb = _expand_mats()
    slot_rows = _slot_cols()
    pad_lane = lambda v: jnp.pad(v.reshape(1, -1), ((0, 0), (0, LANE - v.size)))

    mods = [_ada(cc, ada_w[li], ada_b[li]).reshape(16, 6, 1, d) for li in range(depth)]
    xn = ff = None
    for li in range(depth):
        mod = mods[li]
        w_cat = _build_w_in(w_in[li])
        n1g = norm1_g[li].reshape(1, d)
        if li == 0:
            outs = _in_proj(("split", x, ctx), (mod,), n1g, w_cat, tabs, nb, ntot)
        else:
            outs = _in_proj(("res", xn, ff.reshape(nb, ntot, d)), (mod, mods[li - 1]), n1g, w_cat, tabs, nb, ntot)
        xcur, outs = outs[0], outs[1:]
        zqkv, gate, ab, qb, kb, vb, qc, kc, vc = outs

        conv_w8 = jnp.pad(dn_conv_w[li], ((0, 8 - DN_CONV), (0, 0)))
        q, k, v = _dn_prep(zqkv, conv_w8, ones_a)
        of, ob = _dn_scan(q, k, v, ab, pad_lane(dn_a_log[li]), pad_lane(dn_dt_bias[li]), eg, eb)

        lam_init = 0.8 - 0.6 * math.exp(-0.3 * li)
        yb = _diff_attn(qb, kb, vb, da_lambda[li], jnp.tile(da_subln_g[li], DA_HEADS).reshape(B_W, 1),
                        mean_b, lam_init)
        yw = _win_attn(qc, kc, vc, jnp.pad(wa_sink[li], (0, 8 - WA_HEADS)), n_lat)

        wo = w_out[li]
        wa_o = wo[:A_W].astype(BF16)
        wb_o = wo[A_W:A_W + B_W].astype(BF16)
        wc_o = wo[A_W + B_W:][slot_rows].astype(BF16)
        wr = jnp.pad(jnp.concatenate([router_group_w[li], router_expert_w[li]], axis=1),
                     ((0, 0), (0, LANE - N_GROUPS - N_EXPERTS)))
        br = pad_lane(jnp.concatenate([router_group_b[li], router_expert_b[li]]))
        xn, h2, logits = _out_proj(xcur, of, ob, gate, yb, yw, wa_o, wb_o, wc_o,
                                   jnp.tile(dn_norm_g[li], DN_HEADS).reshape(1, A_W), mean_a, mod,
                                   norm2_g[li].reshape(1, d), wr, br, nb, ntot)

        gw_t, pos_t, cnt = _route(logits.reshape(ntok, LANE))
        counts = cnt[:, 0, :N_GROUPS].reshape(-1)
        ff = _moe(counts, h2.reshape(ntok, d), pos_t, gw_t, exp_w_gate[li].astype(BF16),
                  exp_w_up[li].astype(BF16), exp_w_down[li].astype(BF16))

    return _final(xn, ff.reshape(nb, ntot, d), mods[depth - 1], final_norm_g.reshape(1, d), nb, n_lat, ntot)
```
